```python
import jax, jax.numpy as jnp
from jax import lax
import numpy as np

D_MODEL = 1024
BATCH = 4
SEQ = 8192
DEPTH = 2

CTX_LEN = 256
GRID_W = 64
EPS = 1e-6
ADA_CHUNKS = 9
D_FF = 2816
FFN_RESIDUAL_WEIGHT = 0.5
D_MIX = D_MODEL
MLA_HEADS = 8
NOPE_DIM = 64
ROPE_DIM = 32
AXIS_DIM = ROPE_DIM // 2
QK_HEAD_DIM = NOPE_DIM + ROPE_DIM
V_HEAD_DIM = 64
Q_LORA = 384
KV_LORA = 256
ATT_WIDTH = MLA_HEADS * V_HEAD_DIM
ROPE_BASE = 10000.0
Q_BLOCK = 128
SG_GROUPS = 4
SG_GROUP_DIM = 64
SG_WIDTH = SG_GROUPS * SG_GROUP_DIM
CHUNK = 128
CONV_WIDTH = 256
CONV_K = 3
OFF_KV = Q_LORA
OFF_KPE = OFF_KV + KV_LORA
OFF_SG = OFF_KPE + ROPE_DIM
OFF_CONV = OFF_SG + 2 * SG_WIDTH
MIX_IN_DIM = OFF_CONV + 3 * CONV_WIDTH

kernel_name = 'hybrid_mla_sgu_shortconv_macaron_dit'


def rms_norm(x, g):
    xf = x.astype(jnp.float32)
    y = xf * lax.rsqrt(jnp.mean(xf * xf, axis=-1, keepdims=True) + EPS)
    return (y * g.astype(jnp.float32)).astype(x.dtype)


def modulate(h, shift, scale):
    return h * (1 + scale) + shift


def ffn_half_step(h, shift, scale, gate, g, w_in, w_out):
    a, b = jnp.split(modulate(rms_norm(h, g), shift, scale) @ w_in, 2, axis=-1)
    return h + FFN_RESIDUAL_WEIGHT * gate * ((jax.nn.silu(a) * b) @ w_out)


def axial_rope_tables(n, dtype):
    rows = n // GRID_W
    row = jnp.repeat(jnp.arange(rows), GRID_W).astype(jnp.float32)
    col = jnp.tile(jnp.arange(GRID_W), rows).astype(jnp.float32)
    inv = 1.0 / (ROPE_BASE ** (jnp.arange(0, AXIS_DIM, 2, dtype=jnp.float32) / AXIS_DIM))
    ang_r = row[:, None] * inv
    ang_c = col[:, None] * inv
    ang = jnp.concatenate([ang_r, ang_r, ang_c, ang_c], axis=-1)
    return jnp.cos(ang).astype(dtype), jnp.sin(ang).astype(dtype)


def rotate_half(t):
    a, b = jnp.split(t, 2, axis=-1)
    return jnp.concatenate([-b, a], axis=-1)


def apply_rope(x, rope):
    cos, sin = rope
    x_nope, x_pe = x[..., :NOPE_DIM], x[..., NOPE_DIM:]
    xr, xc = jnp.split(x_pe, 2, axis=-1)
    rot = jnp.concatenate([rotate_half(xr), rotate_half(xc)], axis=-1)
    x_pe = x_pe * cos[None, :, None, :] + rot * sin[None, :, None, :]
    return jnp.concatenate([x_nope, x_pe], axis=-1)


def mla_queries(q_lat, g_q_lat, w_q_up, g_q_head, rope):
    b, n, _ = q_lat.shape
    q = (rms_norm(q_lat, g_q_lat) @ w_q_up).reshape(b, n, MLA_HEADS, QK_HEAD_DIM)
    q = rms_norm(q, g_q_head)
    return apply_rope(q, rope) if rope is not None else q


def mla_keys_values(kv_lat, k_pe, g_kv_lat, w_kv_up, g_k_head, rope):
    b, n, _ = kv_lat.shape
    kv = (rms_norm(kv_lat, g_kv_lat) @ w_kv_up).reshape(b, n, MLA_HEADS, NOPE_DIM + V_HEAD_DIM)
    k_nope, v = kv[..., :NOPE_DIM], kv[..., NOPE_DIM:]
    k_pe = jnp.broadcast_to(k_pe[:, :, None, :], (b, n, MLA_HEADS, ROPE_DIM))
    k = rms_norm(jnp.concatenate([k_nope, k_pe], axis=-1), g_k_head)
    k = apply_rope(k, rope) if rope is not None else k
    return k, v


def softmax_attend(q, k, v):
    s = jnp.einsum('bqhd,bkhd->bhqk', q, k, preferred_element_type=jnp.float32) * (QK_HEAD_DIM ** -0.5)
    p = jax.nn.softmax(s, axis=-1).astype(v.dtype)
    return jnp.einsum('bhqk,bkhd->bqhd', p, v)


def latent_attention(q, k_lat, v_lat, k_ctx, v_ctx):
    b, n, h, dq = q.shape
    k_all = jnp.concatenate([k_lat, k_ctx], axis=1)
    v_all = jnp.concatenate([v_lat, v_ctx], axis=1)
    qb = q.reshape(b, n // Q_BLOCK, Q_BLOCK, h, dq).transpose(1, 0, 2, 3, 4)
    o = lax.map(lambda qblk: softmax_attend(qblk, k_all, v_all), qb)
    return o.transpose(1, 0, 2, 3, 4).reshape(b, n, h * V_HEAD_DIM)


def spatial_gating(sg_in, g_sgu, w_spatial, b_spatial):
    b, n, _ = sg_in.shape
    u, v = jnp.split(jax.nn.gelu(sg_in), 2, axis=-1)
    v = rms_norm(v.reshape(b, n, SG_GROUPS, SG_GROUP_DIM), g_sgu)
    v = v.reshape(b, n // CHUNK, CHUNK, SG_GROUPS, SG_GROUP_DIM)
    vs = jnp.einsum('gpq,bnqgc->bnpgc', w_spatial, v) + b_spatial.T[:, :, None]
    return u * vs.reshape(b, n, SG_WIDTH)


def short_conv(cv_in, w_conv):
    b_gate, c_gate, xin = jnp.split(cv_in, 3, axis=-1)
    z = c_gate * xin
    y = lax.conv_general_dilated(z, w_conv[:, None, :], window_strides=(1,), padding=[(1, 1)],
                                 dimension_numbers=('NWC', 'WIO', 'NWC'), feature_group_count=CONV_WIDTH)
    return b_gate * y


def merge_groups(attn, sg, cv, g_out, w_mix_out):
    y = jnp.concatenate([rms_norm(attn, g_out[:ATT_WIDTH]),
                         rms_norm(sg, g_out[ATT_WIDTH:ATT_WIDTH + SG_WIDTH]),
                         rms_norm(cv, g_out[ATT_WIDTH + SG_WIDTH:])], axis=-1)
    return y @ w_mix_out


def setup_inputs(seed: int = 0) -> dict:
    key = jax.random.key(seed)
    ks = iter(jax.random.split(key, 40))

    def nrm(shape, scale):
        return jax.random.normal(next(ks), shape, jnp.float32) * scale

    def gain(shape):
        return 1.0 + nrm(shape, 0.02)

    L, D = DEPTH, D_MODEL
    return {
        'x': nrm((BATCH, SEQ, D), 1.0),
        'c': nrm((BATCH, D), 1.0),
        'ctx': nrm((BATCH, CTX_LEN, D), 1.0),
        'c_ctx': nrm((D,), 1.0),
        'w_ada': nrm((L, D, ADA_CHUNKS * D), 0.5 * D ** -0.5),
        'b_ada': nrm((L, ADA_CHUNKS * D), 0.02),
        'g_ffn1': gain((L, D)),
        'w_ffn1_in': nrm((L, D, 2 * D_FF), D ** -0.5),
        'w_ffn1_out': nrm((L, D_FF, D), D_FF ** -0.5),
        'g_mix': gain((L, D)),
        'w_mix_in': nrm((L, D, MIX_IN_DIM), D ** -0.5),
        'g_q_lat': gain((L, Q_LORA)),
        'w_q_up': nrm((L, Q_LORA, MLA_HEADS * QK_HEAD_DIM), Q_LORA ** -0.5),
        'g_kv_lat': gain((L, KV_LORA)),
        'w_kv_up': nrm((L, KV_LORA, MLA_HEADS * (NOPE_DIM + V_HEAD_DIM)), KV_LORA ** -0.5),
        'g_q_head': gain((L, QK_HEAD_DIM)),
        'g_k_head': gain((L, QK_HEAD_DIM)),
        'g_sgu': gain((L, SG_GROUPS, SG_GROUP_DIM)),
        'w_spatial': nrm((L, SG_GROUPS, CHUNK, CHUNK), CHUNK ** -0.5),
        'b_spatial': 1.0 + nrm((L, SG_GROUPS, CHUNK), 0.02),
        'w_conv': nrm((L, CONV_K, CONV_WIDTH), CONV_K ** -0.5),
        'g_out': gain((L, D_MIX)),
        'w_mix_out': nrm((L, D_MIX, D), D_MIX ** -0.5),
        'g_ffn2': gain((L, D)),
        'w_ffn2_in': nrm((L, D, 2 * D_FF), D ** -0.5),
        'w_ffn2_out': nrm((L, D_FF, D), D_FF ** -0.5),
    }


def reference(x, c, ctx, c_ctx, w_ada, b_ada, g_ffn1, w_ffn1_in, w_ffn1_out, g_mix, w_mix_in,
              g_q_lat, w_q_up, g_kv_lat, w_kv_up, g_q_head, g_k_head, g_sgu, w_spatial, b_spatial,
              w_conv, g_out, w_mix_out, g_ffn2, w_ffn2_in, w_ffn2_out):
    rope = axial_rope_tables(x.shape[1], x.dtype)
    h, hc = x, ctx
    sc, scc = jax.nn.silu(c), jax.nn.silu(c_ctx)
    splits = [OFF_KV, OFF_KPE, OFF_SG, OFF_CONV]
    for l in range(DEPTH):
        last = l == DEPTH - 1
        mod_l = jnp.split((sc @ w_ada[l] + b_ada[l])[:, None, :], ADA_CHUNKS, axis=-1)
        mod_c = jnp.split((scc @ w_ada[l] + b_ada[l])[None, None, :], ADA_CHUNKS, axis=-1)

        h = ffn_half_step(h, mod_l[0], mod_l[1], mod_l[2], g_ffn1[l], w_ffn1_in[l], w_ffn1_out[l])
        hc = ffn_half_step(hc, mod_c[0], mod_c[1], mod_c[2], g_ffn1[l], w_ffn1_in[l], w_ffn1_out[l])

        hn = modulate(rms_norm(h, g_mix[l]), mod_l[3], mod_l[4])
        hnc = modulate(rms_norm(hc, g_mix[l]), mod_c[3], mod_c[4])
        q_lat, kv_lat, k_pe, sg_in, cv_in = jnp.split(hn @ w_mix_in[l], splits, axis=-1)
        if last:
            kv_lat_c, k_pe_c = jnp.split(hnc @ w_mix_in[l][:, OFF_KV:OFF_SG], [KV_LORA], axis=-1)
        else:
            q_lat_c, kv_lat_c, k_pe_c, sg_in_c, cv_in_c = jnp.split(hnc @ w_mix_in[l], splits, axis=-1)
        q = mla_queries(q_lat, g_q_lat[l], w_q_up[l], g_q_head[l], rope)
        k, v = mla_keys_values(kv_lat, k_pe, g_kv_lat[l], w_kv_up[l], g_k_head[l], rope)
        kc, vc = mla_keys_values(kv_lat_c, k_pe_c, g_kv_lat[l], w_kv_up[l], g_k_head[l], None)
        attn = latent_attention(q, k, v, kc, vc)
        sg = spatial_gating(sg_in, g_sgu[l], w_spatial[l], b_spatial[l])
        cv = short_conv(cv_in, w_conv[l])
        h = h + mod_l[5] * merge_groups(attn, sg, cv, g_out[l], w_mix_out[l])

        h = ffn_half_step(h, mod_l[6], mod_l[7], mod_l[8], g_ffn2[l], w_ffn2_in[l], w_ffn2_out[l])

        if not last:
            qc = mla_queries(q_lat_c, g_q_lat[l], w_q_up[l], g_q_head[l], None)
            b_sz = hc.shape[0]
            attn_c = softmax_attend(qc, kc, vc).reshape(b_sz, hc.shape[1], ATT_WIDTH)
            sg_c = spatial_gating(sg_in_c, g_sgu[l], w_spatial[l], b_spatial[l])
            cv_c = short_conv(cv_in_c, w_conv[l])
            hc = hc + mod_c[5] * merge_groups(attn_c, sg_c, cv_c, g_out[l], w_mix_out[l])
            hc = ffn_half_step(hc, mod_c[6], mod_c[7], mod_c[8], g_ffn2[l], w_ffn2_in[l], w_ffn2_out[l])
    return h
```

```python
import functools

import jax
import jax.numpy as jnp
from jax import lax
from jax.experimental import pallas as pl
from jax.experimental.pallas import tpu as pltpu

EPS = 1e-6
ADA_CHUNKS = 9
FFN_RESIDUAL_WEIGHT = 0.5
GRID_W = 64
ROPE_BASE = 10000.0
MLA_HEADS = 8
NOPE_DIM = 64
ROPE_DIM = 32
AXIS_DIM = ROPE_DIM // 2
QK_HEAD_DIM = NOPE_DIM + ROPE_DIM
V_HEAD_DIM = 64
Q_LORA = 384
KV_LORA = 256
ATT_WIDTH = MLA_HEADS * V_HEAD_DIM
SG_GROUPS = 4
SG_GROUP_DIM = 64
SG_WIDTH = SG_GROUPS * SG_GROUP_DIM
CHUNK = 128
CONV_WIDTH = 256
OFF_KV = Q_LORA
OFF_KPE = OFF_KV + KV_LORA
OFF_SG = OFF_KPE + ROPE_DIM
OFF_CONV = OFF_SG + 2 * SG_WIDTH
LOG2E = 1.4426950408889634

LANES = 128
BF16_ROWS = 16
VMEM_LIMIT_BYTES = 56 * 1024 * 1024

MIX_Q = 0
MIX_KV = MIX_Q + Q_LORA
MIX_SG = MIX_KV + KV_LORA
MIX_CV = MIX_SG + 2 * SG_WIDTH
MIX_KPE = MIX_CV + 3 * CONV_WIDTH
MIX_COLS = MIX_KPE + LANES

FF_CHUNK = 256
TOKEN_TILE = 512
Q_TILE = 512
KV_TILE = 512

BF16 = jnp.bfloat16
F32 = jnp.float32


def _dot(a, b):
    return jnp.dot(a, b, preferred_element_type=F32)


def _rms(x, g):
    return x * lax.rsqrt(jnp.mean(x * x, axis=-1, keepdims=True) + EPS) * g


def _silu(x):
    return x * jax.nn.sigmoid(x)


def _gelu_tanh(x):
    return 0.5 * x * (1.0 + jnp.tanh(0.7978845608028654 * (x + 0.044715 * (x * x * x))))


def _const_spec(shape):
    zeros = (0,) * len(shape)
    return pl.BlockSpec(shape, lambda *_: zeros, pipeline_mode=pl.Buffered(1))


def _ada_kernel(c_ref, w_ref, b_ref, o_ref):
    s = _silu(c_ref[...])
    w = w_ref[0]
    s_hi = s.astype(BF16)
    s_lo = (s - s_hi.astype(F32)).astype(BF16)
    w_hi = w.astype(BF16)
    w_lo = (w - w_hi.astype(F32)).astype(BF16)
    o_ref[0] = _dot(s_hi, w_hi) + _dot(s_lo, w_hi) + _dot(s_hi, w_lo) + b_ref[0]


def _ada(cvec, w_ada, b_ada):
    depth, d, n = w_ada.shape
    rows = cvec.shape[0]
    tn = d
    return pl.pallas_call(
        _ada_kernel,
        grid=(depth, n // tn),
        in_specs=[
            pl.BlockSpec((rows, d), lambda l, j: (0, 0)),
            pl.BlockSpec((1, d, tn), lambda l, j: (l, 0, j)),
            pl.BlockSpec((1, 1, tn), lambda l, j: (l, 0, j)),
        ],
        out_specs=pl.BlockSpec((1, rows, tn), lambda l, j: (l, 0, j)),
        out_shape=jax.ShapeDtypeStruct((depth, rows, n), F32),
        compiler_params=pltpu.CompilerParams(vmem_limit_bytes=VMEM_LIMIT_BYTES),
        name="ada_modulation",
    )(cvec, w_ada, b_ada.reshape(depth, 1, n))


def _ffn_kernel(*refs, merge, mod_base, n_chunks):
    if merge:
        (x_ref, mod_ref, attn_ref, sgcv_ref, gatt_ref, wmo_ref, g_ref, wab_ref, wout_ref,
         o_ref, h_scr, acc_scr) = refs
    else:
        x_ref, mod_ref, g_ref, wab_ref, wout_ref, o_ref, h_scr, acc_scr = refs
    x = x_ref[0]
    if merge:
        att = _rms(attn_ref[0], gatt_ref[...]).astype(BF16)
        y = _dot(att, wmo_ref[0:ATT_WIDTH, :]) + _dot(sgcv_ref[0], wmo_ref[ATT_WIDTH:, :])
        x = x + mod_ref[0, 5:6, :] * y
    o_ref[0] = x
    shift = mod_ref[0, mod_base:mod_base + 1, :]
    scale = mod_ref[0, mod_base + 1:mod_base + 2, :]
    h_scr[...] = (_rms(x, g_ref[...]) * (1.0 + scale) + shift).astype(BF16)

    def chunk(c):
        ab = _dot(h_scr[...], wab_ref[c])
        a = ab[:, :FF_CHUNK]
        b = ab[:, FF_CHUNK:]
        return _dot((_silu(a) * b).astype(BF16), wout_ref[c])

    acc_scr[...] = chunk(0)

    def body(c, carry):
        acc_scr[...] += chunk(c)
        return carry

    lax.fori_loop(1, n_chunks, body, 0)
    gate = mod_ref[0, mod_base + 2:mod_base + 3, :]
    o_ref[0] = o_ref[0] + (FFN_RESIDUAL_WEIGHT * gate) * acc_scr[...]


def _ffn(x, mod, g, wab, wout, *, mod_base, merge_args=None):
    bsz, t, d = x.shape
    tm = min(TOKEN_TILE, t)
    n_chunks = wab.shape[0]
    tile = lambda w: pl.BlockSpec((1, tm, w), lambda b, i: (b, i, 0))
    in_specs = [tile(d), pl.BlockSpec((1, ADA_CHUNKS, d), lambda b, i: (b, 0, 0))]
    args = [x, mod]
    if merge_args is not None:
        attn, sgcv, g_att, wmo = merge_args
        in_specs += [tile(ATT_WIDTH), tile(SG_WIDTH + CONV_WIDTH), _const_spec(g_att.shape), _const_spec(wmo.shape)]
        args += [attn, sgcv, g_att, wmo]
    in_specs += [_const_spec(g.shape), _const_spec(wab.shape), _const_spec(wout.shape)]
    args += [g, wab, wout]
    return pl.pallas_call(
        functools.partial(_ffn_kernel, merge=merge_args is not None, mod_base=mod_base, n_chunks=n_chunks),
        grid=(bsz, t // tm),
        in_specs=in_specs,
        out_specs=tile(d),
        out_shape=jax.ShapeDtypeStruct(x.shape, F32),
        scratch_shapes=[pltpu.VMEM((tm, d), BF16), pltpu.VMEM((tm, d), F32)],
        compiler_params=pltpu.CompilerParams(vmem_limit_bytes=VMEM_LIMIT_BYTES),
        name="ffn_merge" if merge_args is not None else "ffn",
    )(*args)


def _rope(xg, cos, sin_next, sin_prev):
    return xg * cos + pltpu.roll(xg, LANES - 8, 1) * sin_next + pltpu.roll(xg, 8, 1) * sin_prev


def _mix_kernel(x_ref, xp_ref, xn_ref, mod_ref, gmix_ref, w_ref, gql_ref, wq_ref, gkvl_ref, wkv_ref,
                gqs_ref, gks_ref, gkpe_ref, cos_ref, sn_ref, sp_ref, gsgu_ref, wsp_ref, bsp_ref,
                wcv_ref, gosg_ref, gocv_ref, q_ref, k_ref, v_ref, sgcv_ref, h_scr, z_scr, *, tm, q_scale):
    i = pl.program_id(1)
    last = pl.num_programs(1) - 1
    halo = BF16_ROWS
    shift = mod_ref[0, 3:4, :]
    scale1 = 1.0 + mod_ref[0, 4:5, :]
    gmix = gmix_ref[...]

    def norm_mod(x):
        return (_rms(x, gmix) * scale1 + shift).astype(BF16)

    h_scr[0:halo] = norm_mod(xp_ref[0])
    h_scr[halo:halo + tm] = norm_mod(x_ref[0])
    h_scr[halo + tm:] = norm_mod(xn_ref[0])
    hm = h_scr[halo:halo + tm]

    cos = cos_ref[...]
    sn = sn_ref[...]
    sp = sp_ref[...]

    qn = _rms(_dot(hm, w_ref[:, MIX_Q:MIX_Q + Q_LORA]), gql_ref[...]).astype(BF16)
    qs = _dot(qn, wq_ref[...])
    for h in range(MLA_HEADS):
        sl = slice(h * LANES, (h + 1) * LANES)
        xq = qs[:, sl]
        r = lax.rsqrt(jnp.sum(xq * xq, axis=-1, keepdims=True) * (1.0 / QK_HEAD_DIM) + EPS)
        y = _rope(xq * gqs_ref[:, sl], cos, sn, sp)
        q_ref[0, h] = (y * (r * q_scale)).astype(BF16)

    kvn = _rms(_dot(hm, w_ref[:, MIX_KV:MIX_KV + KV_LORA]), gkvl_ref[...]).astype(BF16)
    kv = _dot(kvn, wkv_ref[...])
    v_ref[0] = kv[:, MLA_HEADS * LANES:].astype(BF16)
    kpe = _dot(hm, w_ref[:, MIX_KPE:MIX_KPE + LANES])
    ss_pe = jnp.sum(kpe * kpe, axis=-1, keepdims=True)
    pe = _rope(kpe * gkpe_ref[...], cos, sn, sp)
    for h in range(MLA_HEADS):
        sl = slice(h * LANES, (h + 1) * LANES)
        xk = kv[:, sl]
        r = lax.rsqrt((jnp.sum(xk * xk, axis=-1, keepdims=True) + ss_pe) * (1.0 / QK_HEAD_DIM) + EPS)
        k_ref[0, h] = ((xk * gks_ref[:, sl] + pe) * r).astype(BF16)

    ge = _gelu_tanh(_dot(hm, w_ref[:, MIX_SG:MIX_SG + 2 * SG_WIDTH]))
    u = ge[:, :SG_WIDTH]
    lane = lax.broadcasted_iota(jnp.int32, (1, LANES), 1)
    low = lane < SG_GROUP_DIM
    vparts = []
    for c in range(SG_WIDTH // LANES):
        vv = ge[:, SG_WIDTH + c * LANES:SG_WIDTH + (c + 1) * LANES]
        v2 = vv * vv
        s_lo = jnp.sum(jnp.where(low, v2, 0.0), axis=-1, keepdims=True)
        s_hi = jnp.sum(jnp.where(low, 0.0, v2), axis=-1, keepdims=True)
        r = jnp.where(low, lax.rsqrt(s_lo * (1.0 / SG_GROUP_DIM) + EPS), lax.rsqrt(s_hi * (1.0 / SG_GROUP_DIM) + EPS))
        vparts.append(vv * r * gsgu_ref[:, c * LANES:(c + 1) * LANES])
    vn = jnp.concatenate(vparts, axis=-1)
    group = lax.broadcasted_iota(jnp.int32, (1, SG_WIDTH), 1) // SG_GROUP_DIM
    sg_rows = []
    for c in range(tm // CHUNK):
        vc = vn[c * CHUNK:(c + 1) * CHUNK]
        stacked = jnp.concatenate([jnp.where(group == g, vc, 0.0) for g in range(SG_GROUPS)], axis=0)
        vs = _dot(wsp_ref[...], stacked.astype(BF16)) + bsp_ref[...]
        sg_rows.append(u[c * CHUNK:(c + 1) * CHUNK] * vs)
    sg = jnp.concatenate(sg_rows, axis=0)
    sgcv_ref[0, :, 0:SG_WIDTH] = _rms(sg, gosg_ref[...]).astype(BF16)

    cvin = _dot(h_scr[...], w_ref[:, MIX_CV:MIX_CV + 3 * CONV_WIDTH])
    z = cvin[:, CONV_WIDTH:2 * CONV_WIDTH] * cvin[:, 2 * CONV_WIDTH:]
    z_scr[...] = z
    z_scr[0:halo] = jnp.where(i > 0, z[0:halo], 0.0)
    z_scr[halo + tm:] = jnp.where(i < last, z[halo + tm:], 0.0)
    y = (wcv_ref[0:1, :] * z_scr[halo - 1:halo - 1 + tm] + wcv_ref[1:2, :] * z_scr[halo:halo + tm]
         + wcv_ref[2:3, :] * z_scr[halo + 1:halo + 1 + tm])
    cv = cvin[halo:halo + tm, 0:CONV_WIDTH] * y
    sgcv_ref[0, :, SG_WIDTH:] = _rms(cv, gocv_ref[...]).astype(BF16)


def _mix(x, mod, p, tables):
    bsz, t, d = x.shape
    tm = min(TOKEN_TILE, t)
    halo = BF16_ROWS
    nblk = t // halo
    per = tm // halo
    cos, sn, sp = tables
    const_args = [p["g_mix"], p["w_mix"], p["g_q_lat"], p["w_q"], p["g_kv_lat"], p["w_kv"],
                  p["g_q_slot"], p["g_k_slot"], p["g_k_pe"]]
    tail_args = [p["g_sgu"], p["w_sp"], p["b_sp"], p["w_conv"], p["g_out_sg"], p["g_out_cv"]]
    tab_spec = pl.BlockSpec((tm, LANES), lambda b, i: (i, 0))
    in_specs = ([pl.BlockSpec((1, tm, d), lambda b, i: (b, i, 0)),
                 pl.BlockSpec((1, halo, d), lambda b, i: (b, jnp.maximum(i * per - 1, 0), 0)),
                 pl.BlockSpec((1, halo, d), lambda b, i: (b, jnp.minimum((i + 1) * per, nblk - 1), 0)),
                 pl.BlockSpec((1, ADA_CHUNKS, d), lambda b, i: (b, 0, 0))]
                + [_const_spec(a.shape) for a in const_args]
                + [tab_spec, tab_spec, tab_spec]
                + [_const_spec(a.shape) for a in tail_args])
    head_spec = pl.BlockSpec((1, MLA_HEADS, tm, LANES), lambda b, i: (b, 0, i, 0))
    return pl.pallas_call(
        functools.partial(_mix_kernel, tm=tm, q_scale=LOG2E * QK_HEAD_DIM ** -0.5),
        grid=(bsz, t // tm),
        in_specs=in_specs,
        out_specs=[head_spec, head_spec,
                   pl.BlockSpec((1, tm, ATT_WIDTH), lambda b, i: (b, i, 0)),
                   pl.BlockSpec((1, tm, SG_WIDTH + CONV_WIDTH), lambda b, i: (b, i, 0))],
        out_shape=[jax.ShapeDtypeStruct((bsz, MLA_HEADS, t, LANES), BF16),
                   jax.ShapeDtypeStruct((bsz, MLA_HEADS, t, LANES), BF16),
                   jax.ShapeDtypeStruct((bsz, t, ATT_WIDTH), BF16),
                   jax.ShapeDtypeStruct((bsz, t, SG_WIDTH + CONV_WIDTH), BF16)],
        scratch_shapes=[pltpu.VMEM((tm + 2 * halo, d), BF16), pltpu.VMEM((tm + 2 * halo, CONV_WIDTH), F32)],
        compiler_params=pltpu.CompilerParams(vmem_limit_bytes=VMEM_LIMIT_BYTES),
        name="mixer_front",
    )(x, x, x, mod, *const_args, cos, sn, sp, *tail_args)


NEG_BIG = -1e30


def _attn_kernel(*refs, tk, has_extra):
    if has_extra:
        q_ref, k_ref, v_ref, k2_ref, v2_ref, o_ref = refs
    else:
        q_ref, k_ref, v_ref, o_ref = refs
    tq = q_ref.shape[2]
    qs = [q_ref[0, h] for h in range(2)]

    def step(qh, kh, vblk, state):
        m, l, acc = state
        s = lax.dot_general(qh, kh, (((1,), (1,)), ((), ())), preferred_element_type=F32)
        m_new = jnp.maximum(m, jnp.max(s, axis=-1, keepdims=True))
        alpha = jnp.exp2(m - m_new)
        p = jnp.exp2(s - m_new)
        l = alpha * l + jnp.sum(p, axis=-1, keepdims=True)
        acc = alpha * acc + _dot(p.astype(BF16), vblk)
        return m_new, l, acc

    init = (jnp.full((tq, 1), NEG_BIG, F32), jnp.zeros((tq, 1), F32), jnp.zeros((tq, LANES), F32))

    def body(j, states):
        off = pl.multiple_of(j * tk, tk)
        vblk = v_ref[0, pl.ds(off, tk), :]
        return tuple(step(qs[h], k_ref[0, h, pl.ds(off, tk), :], vblk, states[h]) for h in range(2))

    states = lax.fori_loop(0, k_ref.shape[2] // tk, body, (init, init))
    if has_extra:
        states = tuple(step(qs[h], k2_ref[0, h], v2_ref[0], states[h]) for h in range(2))
    outs = [acc / l for (_, l, acc) in states]
    lane = lax.broadcasted_iota(jnp.int32, (1, LANES), 1)
    o_ref[0] = jnp.where(lane < V_HEAD_DIM, outs[0], outs[1])


def _attention(q, k, v, k2=None, v2=None):
    bsz, heads, t, _ = q.shape
    t1 = k.shape[2]
    tq = min(Q_TILE, t)
    tk = min(KV_TILE, t1)
    has_extra = k2 is not None
    in_specs = [pl.BlockSpec((1, 2, tq, LANES), lambda b, p, i: (b, p, i, 0)),
                pl.BlockSpec((1, 2, t1, LANES), lambda b, p, i: (b, p, 0, 0)),
                pl.BlockSpec((1, t1, LANES), lambda b, p, i: (b, 0, p))]
    args = [q, k, v]
    if has_extra:
        t2 = k2.shape[2]
        in_specs += [pl.BlockSpec((1, 2, t2, LANES), lambda b, p, i: (b, p, 0, 0)),
                     pl.BlockSpec((1, t2, LANES), lambda b, p, i: (b, 0, p))]
        args += [k2, v2]
    return pl.pallas_call(
        functools.partial(_attn_kernel, tk=tk, has_extra=has_extra),
        grid=(bsz, heads // 2, t // tq),
        in_specs=in_specs,
        out_specs=pl.BlockSpec((1, tq, LANES), lambda b, p, i: (b, i, p)),
        out_shape=jax.ShapeDtypeStruct((bsz, t, ATT_WIDTH), F32),
        compiler_params=pltpu.CompilerParams(vmem_limit_bytes=VMEM_LIMIT_BYTES),
        name="attention_extra" if has_extra else "attention",
    )(*args)


def _slot_pad(w, width):
    lead = w.shape[:-1]
    w = w.reshape(*lead, MLA_HEADS, width)
    w = jnp.pad(w, [(0, 0)] * len(lead) + [(0, 0), (0, LANES - width)])
    return w.reshape(*lead, MLA_HEADS * LANES)


def _prep_ffn(w_in, w_out):
    d, two_ff = w_in.shape
    d_ff = two_ff // 2
    n = d_ff // FF_CHUNK
    wa = w_in[:, :d_ff].reshape(d, n, FF_CHUNK)
    wb = w_in[:, d_ff:].reshape(d, n, FF_CHUNK)
    wab = jnp.concatenate([wa, wb], axis=-1).transpose(1, 0, 2).astype(BF16)
    return wab, w_out.reshape(n, FF_CHUNK, d).astype(BF16)


def _prep_layer(l, w):
    row = lambda a: a.reshape(1, -1).astype(F32)
    w_mix_in = w["w_mix_in"][l]
    d = w_mix_in.shape[0]
    kpe_cols = jnp.zeros((d, LANES), F32).at[:, NOPE_DIM:QK_HEAD_DIM].set(w_mix_in[:, OFF_KPE:OFF_SG])
    w_mix = jnp.concatenate([w_mix_in[:, :OFF_KPE], w_mix_in[:, OFF_SG:], kpe_cols], axis=1).astype(BF16)
    w_kv = w["w_kv_up"][l].reshape(KV_LORA, MLA_HEADS, NOPE_DIM + V_HEAD_DIM)
    w_k = _slot_pad(w_kv[:, :, :NOPE_DIM].reshape(KV_LORA, -1), NOPE_DIM)
    w_v = w_kv[:, :, NOPE_DIM:].reshape(KV_LORA, ATT_WIDTH)
    g_k = w["g_k_head"][l]
    g_out = w["g_out"][l]
    w_sp = w["w_spatial"][l]
    ffn1 = _prep_ffn(w["w_ffn1_in"][l], w["w_ffn1_out"][l])
    ffn2 = _prep_ffn(w["w_ffn2_in"][l], w["w_ffn2_out"][l])
    return {
        "g_ffn1": row(w["g_ffn1"][l]), "ffn1_ab": ffn1[0], "ffn1_out": ffn1[1],
        "g_ffn2": row(w["g_ffn2"][l]), "ffn2_ab": ffn2[0], "ffn2_out": ffn2[1],
        "g_mix": row(w["g_mix"][l]), "w_mix": w_mix,
        "g_q_lat": row(w["g_q_lat"][l]), "w_q": _slot_pad(w["w_q_up"][l], QK_HEAD_DIM).astype(BF16),
        "g_kv_lat": row(w["g_kv_lat"][l]), "w_kv": jnp.concatenate([w_k, w_v], axis=1).astype(BF16),
        "g_q_slot": row(_slot_pad(jnp.tile(w["g_q_head"][l], MLA_HEADS), QK_HEAD_DIM)),
        "g_k_slot": row(_slot_pad(jnp.tile(g_k[:NOPE_DIM], MLA_HEADS), NOPE_DIM)),
        "g_k_pe": row(jnp.zeros((LANES,), F32).at[NOPE_DIM:QK_HEAD_DIM].set(g_k[NOPE_DIM:])),
        "g_sgu": row(w["g_sgu"][l]),
        "w_sp": jnp.concatenate([w_sp[g] for g in range(SG_GROUPS)], axis=1).astype(BF16),
        "b_sp": jnp.repeat(w["b_spatial"][l].T, SG_GROUP_DIM, axis=1).astype(F32),
        "w_conv": w["w_conv"][l].astype(F32),
        "g_out_att": row(g_out[:ATT_WIDTH]),
        "g_out_sg": row(g_out[ATT_WIDTH:ATT_WIDTH + SG_WIDTH]),
        "g_out_cv": row(g_out[ATT_WIDTH + SG_WIDTH:]),
        "w_mix_out": w["w_mix_out"][l].astype(BF16),
    }


def _rope_tables(n):
    rows = n // GRID_W
    row = jnp.repeat(jnp.arange(rows), GRID_W).astype(F32)
    col = jnp.tile(jnp.arange(GRID_W), rows).astype(F32)
    inv = 1.0 / (ROPE_BASE ** (jnp.arange(0, AXIS_DIM, 2, dtype=F32) / AXIS_DIM))
    ang_r = row[:, None] * inv
    ang_c = col[:, None] * inv
    ang = jnp.concatenate([ang_r, ang_r, ang_c, ang_c], axis=-1)
    cos, sin = jnp.cos(ang), jnp.sin(ang)
    first_half = (jnp.arange(ROPE_DIM) % AXIS_DIM) < AXIS_DIM // 2
    pad = lambda a, fill: jnp.pad(a, ((0, 0), (NOPE_DIM, LANES - QK_HEAD_DIM)), constant_values=fill)
    return (pad(cos, 1.0), pad(jnp.where(first_half, -sin, 0.0), 0.0), pad(jnp.where(first_half, 0.0, sin), 0.0))


def _identity_tables(n):
    return (jnp.ones((n, LANES), F32), jnp.zeros((n, LANES), F32), jnp.zeros((n, LANES), F32))


def kernel(x, c, ctx, c_ctx, w_ada, b_ada, g_ffn1, w_ffn1_in, w_ffn1_out, g_mix, w_mix_in, g_q_lat, w_q_up, g_kv_lat, w_kv_up, g_q_head, g_k_head, g_sgu, w_spatial, b_spatial, w_conv, g_out, w_mix_out, g_ffn2, w_ffn2_in, w_ffn2_out):
    weights = dict(g_ffn1=g_ffn1, w_ffn1_in=w_ffn1_in, w_ffn1_out=w_ffn1_out, g_mix=g_mix, w_mix_in=w_mix_in,
                   g_q_lat=g_q_lat, w_q_up=w_q_up, g_kv_lat=g_kv_lat, w_kv_up=w_kv_up, g_q_head=g_q_head,
                   g_k_head=g_k_head, g_sgu=g_sgu, w_spatial=w_spatial, b_spatial=b_spatial, w_conv=w_conv,
                   g_out=g_out, w_mix_out=w_mix_out, g_ffn2=g_ffn2, w_ffn2_in=w_ffn2_in, w_ffn2_out=w_ffn2_out)
    bsz, t, d = x.shape
    depth = w_ada.shape[0]
    rows = 8 * ((bsz + 1 + 7) // 8)
    cvec = jnp.zeros((rows, d), F32).at[:bsz].set(c).at[bsz].set(c_ctx)
    mod = _ada(cvec, w_ada, b_ada)
    lat_tables = _rope_tables(t)
    ctx_tables = _identity_tables(ctx.shape[1])

    h, hc = x, ctx
    for l in range(depth):
        last = l == depth - 1
        p = _prep_layer(l, weights)
        mod_l = mod[l, :bsz].reshape(bsz, ADA_CHUNKS, d)
        mod_c = jnp.broadcast_to(mod[l, bsz].reshape(1, ADA_CHUNKS, d), (bsz, ADA_CHUNKS, d))

        h = _ffn(h, mod_l, p["g_ffn1"], p["ffn1_ab"], p["ffn1_out"], mod_base=0)
        hc = _ffn(hc, mod_c, p["g_ffn1"], p["ffn1_ab"], p["ffn1_out"], mod_base=0)

        q, k, v, sgcv = _mix(h, mod_l, p, lat_tables)
        qc, kc, vc, sgcv_c = _mix(hc, mod_c, p, ctx_tables)
        attn = _attention(q, k, v, kc, vc)
        h = _ffn(h, mod_l, p["g_ffn2"], p["ffn2_ab"], p["ffn2_out"], mod_base=6,
                 merge_args=(attn, sgcv, p["g_out_att"], p["w_mix_out"]))
        if not last:
            attn_c = _attention(qc, kc, vc)
            hc = _ffn(hc, mod_c, p["g_ffn2"], p["ffn2_ab"], p["ffn2_out"], mod_base=6,
                      merge_args=(attn_c, sgcv_c, p["g_out_att"], p["w_mix_out"]))
    return h
```

```python
import functools

import jax
import jax.numpy as jnp
from jax import lax
from jax.experimental import pallas as pl
from jax.experimental.pallas import tpu as pltpu

EPS = 1e-6
ADA_CHUNKS = 9
FFN_RESIDUAL_WEIGHT = 0.5
GRID_W = 64
ROPE_BASE = 10000.0
MLA_HEADS = 8
NOPE_DIM = 64
ROPE_DIM = 32
AXIS_DIM = ROPE_DIM // 2
QK_HEAD_DIM = NOPE_DIM + ROPE_DIM
V_HEAD_DIM = 64
Q_LORA = 384
KV_LORA = 256
ATT_WIDTH = MLA_HEADS * V_HEAD_DIM
SG_GROUPS = 4
SG_GROUP_DIM = 64
SG_WIDTH = SG_GROUPS * SG_GROUP_DIM
CHUNK = 128
CONV_WIDTH = 256
OFF_KV = Q_LORA
OFF_KPE = OFF_KV + KV_LORA
OFF_SG = OFF_KPE + ROPE_DIM
OFF_CONV = OFF_SG + 2 * SG_WIDTH
LOG2E = 1.4426950408889634
SHIFT_LANE = QK_HEAD_DIM
MAX_SAFE_SHIFT = 60.0

LANES = 128
BF16_ROWS = 16
VMEM_LIMIT_BYTES = 56 * 1024 * 1024

MIX_Q = 0
MIX_KV = MIX_Q + Q_LORA
MIX_SG = MIX_KV + KV_LORA
MIX_CV = MIX_SG + 2 * SG_WIDTH
MIX_KPE = MIX_CV + 3 * CONV_WIDTH
MIX_COLS = MIX_KPE + LANES

FF_CHUNK = 256
TOKEN_TILE = 512
Q_TILE = 512
KV_TILE = 512

BF16 = jnp.bfloat16
F32 = jnp.float32


def _dot(a, b):
    return jnp.dot(a, b, preferred_element_type=F32)


def _rms(x, g):
    return x * lax.rsqrt(jnp.mean(x * x, axis=-1, keepdims=True) + EPS) * g


def _silu(x):
    return x * jax.nn.sigmoid(x)


def _gelu_tanh(x):
    return 0.5 * x * (1.0 + jnp.tanh(0.7978845608028654 * (x + 0.044715 * (x * x * x))))


def _const_spec(shape):
    zeros = (0,) * len(shape)
    return pl.BlockSpec(shape, lambda *_: zeros, pipeline_mode=pl.Buffered(1))


def _ada_kernel(c_ref, w_ref, b_ref, o_ref):
    s = _silu(c_ref[...])
    w = w_ref[0]
    s_hi = s.astype(BF16)
    s_lo = (s - s_hi.astype(F32)).astype(BF16)
    w_hi = w.astype(BF16)
    w_lo = (w - w_hi.astype(F32)).astype(BF16)
    o_ref[0] = _dot(s_hi, w_hi) + _dot(s_lo, w_hi) + _dot(s_hi, w_lo) + b_ref[0]


def _ada(cvec, w_ada, b_ada):
    depth, d, n = w_ada.shape
    rows = cvec.shape[0]
    tn = d
    return pl.pallas_call(
        _ada_kernel,
        grid=(depth, n // tn),
        in_specs=[
            pl.BlockSpec((rows, d), lambda l, j: (0, 0)),
            pl.BlockSpec((1, d, tn), lambda l, j: (l, 0, j)),
            pl.BlockSpec((1, 1, tn), lambda l, j: (l, 0, j)),
        ],
        out_specs=pl.BlockSpec((1, rows, tn), lambda l, j: (l, 0, j)),
        out_shape=jax.ShapeDtypeStruct((depth, rows, n), F32),
        compiler_params=pltpu.CompilerParams(vmem_limit_bytes=VMEM_LIMIT_BYTES),
        name="ada_modulation",
    )(cvec, w_ada, b_ada.reshape(depth, 1, n))


def _ffn_kernel(*refs, merge, mod_base, n_chunks):
    if merge:
        (x_ref, mod_ref, attn_ref, sgcv_ref, gatt_ref, wmo_ref, g_ref, wab_ref, wout_ref,
         o_ref, h_scr, acc_scr) = refs
    else:
        x_ref, mod_ref, g_ref, wab_ref, wout_ref, o_ref, h_scr, acc_scr = refs
    x = x_ref[0]
    if merge:
        att = _rms(attn_ref[0], gatt_ref[...]).astype(BF16)
        y = _dot(att, wmo_ref[0:ATT_WIDTH, :]) + _dot(sgcv_ref[0], wmo_ref[ATT_WIDTH:, :])
        x = x + mod_ref[0, 5:6, :] * y
    o_ref[0] = x
    shift = mod_ref[0, mod_base:mod_base + 1, :]
    scale = mod_ref[0, mod_base + 1:mod_base + 2, :]
    h_scr[...] = (_rms(x, g_ref[...]) * (1.0 + scale) + shift).astype(BF16)

    def chunk(c):
        ab = _dot(h_scr[...], wab_ref[c])
        a = ab[:, :FF_CHUNK]
        b = ab[:, FF_CHUNK:]
        return _dot((_silu(a) * b).astype(BF16), wout_ref[c])

    acc_scr[...] = chunk(0)

    def body(c, carry):
        acc_scr[...] += chunk(c)
        return carry

    lax.fori_loop(1, n_chunks, body, 0)
    gate = mod_ref[0, mod_base + 2:mod_base + 3, :]
    o_ref[0] = o_ref[0] + (FFN_RESIDUAL_WEIGHT * gate) * acc_scr[...]


def _ffn(x, mod, g, wab, wout, *, mod_base, merge_args=None):
    bsz, t, d = x.shape
    tm = min(TOKEN_TILE, t)
    n_chunks = wab.shape[0]
    tile = lambda w: pl.BlockSpec((1, tm, w), lambda b, i: (b, i, 0))
    in_specs = [tile(d), pl.BlockSpec((1, ADA_CHUNKS, d), lambda b, i: (b, 0, 0))]
    args = [x, mod]
    if merge_args is not None:
        attn, sgcv, g_att, wmo = merge_args
        in_specs += [tile(ATT_WIDTH), tile(SG_WIDTH + CONV_WIDTH), _const_spec(g_att.shape), _const_spec(wmo.shape)]
        args += [attn, sgcv, g_att, wmo]
    in_specs += [_const_spec(g.shape), _const_spec(wab.shape), _const_spec(wout.shape)]
    args += [g, wab, wout]
    return pl.pallas_call(
        functools.partial(_ffn_kernel, merge=merge_args is not None, mod_base=mod_base, n_chunks=n_chunks),
        grid=(bsz, t // tm),
        in_specs=in_specs,
        out_specs=tile(d),
        out_shape=jax.ShapeDtypeStruct(x.shape, F32),
        scratch_shapes=[pltpu.VMEM((tm, d), BF16), pltpu.VMEM((tm, d), F32)],
        compiler_params=pltpu.CompilerParams(vmem_limit_bytes=VMEM_LIMIT_BYTES),
        name="ffn_merge" if merge_args is not None else "ffn",
    )(*args)


def _rope(xg, cos, sin_next, sin_prev):
    return xg * cos + pltpu.roll(xg, LANES - 8, 1) * sin_next + pltpu.roll(xg, 8, 1) * sin_prev


def _mix_kernel(x_ref, xp_ref, xn_ref, mod_ref, gmix_ref, w_ref, gql_ref, wq_ref, gkvl_ref, wkv_ref,
                gqs_ref, gks_ref, gkpe_ref, kb_ref, cos_ref, sn_ref, sp_ref, gsgu_ref, wsp_ref, bsp_ref,
                wcv_ref, gosg_ref, gocv_ref, q_ref, k_ref, v_ref, sgcv_ref, h_scr, z_scr, *, tm, q_scale):
    i = pl.program_id(1)
    last = pl.num_programs(1) - 1
    halo = BF16_ROWS
    shift = mod_ref[0, 3:4, :]
    scale1 = 1.0 + mod_ref[0, 4:5, :]
    gmix = gmix_ref[...]

    def norm_mod(x):
        return (_rms(x, gmix) * scale1 + shift).astype(BF16)

    h_scr[0:halo] = norm_mod(xp_ref[0])
    h_scr[halo:halo + tm] = norm_mod(x_ref[0])
    h_scr[halo + tm:] = norm_mod(xn_ref[0])
    hm = h_scr[halo:halo + tm]

    cos = cos_ref[...]
    sn = sn_ref[...]
    sp = sp_ref[...]

    qn = _rms(_dot(hm, w_ref[:, MIX_Q:MIX_Q + Q_LORA]), gql_ref[...]).astype(BF16)
    qs = _dot(qn, wq_ref[...])
    lane = lax.broadcasted_iota(jnp.int32, (1, LANES), 1)
    for h in range(MLA_HEADS):
        sl = slice(h * LANES, (h + 1) * LANES)
        xq = qs[:, sl]
        r = lax.rsqrt(jnp.sum(xq * xq, axis=-1, keepdims=True) * (1.0 / QK_HEAD_DIM) + EPS)
        y = _rope(xq * gqs_ref[:, sl], cos, sn, sp) * (r * q_scale)
        shift_c = jnp.sqrt(jnp.sum(y * y, axis=-1, keepdims=True)) * kb_ref[...]
        q_ref[0, h] = jnp.where(lane == SHIFT_LANE, -shift_c, y).astype(BF16)

    kvn = _rms(_dot(hm, w_ref[:, MIX_KV:MIX_KV + KV_LORA]), gkvl_ref[...]).astype(BF16)
    kv = _dot(kvn, wkv_ref[...])
    kpe = _dot(hm, w_ref[:, MIX_KPE:MIX_KPE + LANES])
    ss_pe = jnp.sum(kpe * kpe, axis=-1, keepdims=True)
    pe = _rope(kpe * gkpe_ref[...], cos, sn, sp)
    for h in range(MLA_HEADS):
        sl = slice(h * LANES, (h + 1) * LANES)
        xk = kv[:, sl]
        r = lax.rsqrt((jnp.sum(xk * xk, axis=-1, keepdims=True) + ss_pe) * (1.0 / QK_HEAD_DIM) + EPS)
        k_ref[0, h] = jnp.where(lane == SHIFT_LANE, 1.0, (xk * gks_ref[:, sl] + pe) * r).astype(BF16)
        xv = kv[:, (MLA_HEADS + h) * LANES:(MLA_HEADS + h + 1) * LANES]
        v_ref[0, h] = jnp.where(lane == V_HEAD_DIM, 1.0, xv).astype(BF16)

    ge = _gelu_tanh(_dot(hm, w_ref[:, MIX_SG:MIX_SG + 2 * SG_WIDTH]))
    u = ge[:, :SG_WIDTH]
    low = lane < SG_GROUP_DIM
    vparts = []
    for c in range(SG_WIDTH // LANES):
        vv = ge[:, SG_WIDTH + c * LANES:SG_WIDTH + (c + 1) * LANES]
        v2 = vv * vv
        s_lo = jnp.sum(jnp.where(low, v2, 0.0), axis=-1, keepdims=True)
        s_hi = jnp.sum(jnp.where(low, 0.0, v2), axis=-1, keepdims=True)
        r = jnp.where(low, lax.rsqrt(s_lo * (1.0 / SG_GROUP_DIM) + EPS), lax.rsqrt(s_hi * (1.0 / SG_GROUP_DIM) + EPS))
        vparts.append(vv * r * gsgu_ref[:, c * LANES:(c + 1) * LANES])
    vn = jnp.concatenate(vparts, axis=-1)
    group = lax.broadcasted_iota(jnp.int32, (1, SG_WIDTH), 1) // SG_GROUP_DIM
    sg_rows = []
    for c in range(tm // CHUNK):
        vc = vn[c * CHUNK:(c + 1) * CHUNK]
        stacked = jnp.concatenate([jnp.where(group == g, vc, 0.0) for g in range(SG_GROUPS)], axis=0)
        vs = _dot(wsp_ref[...], stacked.astype(BF16)) + bsp_ref[...]
        sg_rows.append(u[c * CHUNK:(c + 1) * CHUNK] * vs)
    sg = jnp.concatenate(sg_rows, axis=0)
    sgcv_ref[0, :, 0:SG_WIDTH] = _rms(sg, gosg_ref[...]).astype(BF16)

    cvin = _dot(h_scr[...], w_ref[:, MIX_CV:MIX_CV + 3 * CONV_WIDTH])
    z = cvin[:, CONV_WIDTH:2 * CONV_WIDTH] * cvin[:, 2 * CONV_WIDTH:]
    z_scr[...] = z
    z_scr[0:halo] = jnp.where(i > 0, z[0:halo], 0.0)
    z_scr[halo + tm:] = jnp.where(i < last, z[halo + tm:], 0.0)
    y = (wcv_ref[0:1, :] * z_scr[halo - 1:halo - 1 + tm] + wcv_ref[1:2, :] * z_scr[halo:halo + tm]
         + wcv_ref[2:3, :] * z_scr[halo + 1:halo + 1 + tm])
    cv = cvin[halo:halo + tm, 0:CONV_WIDTH] * y
    sgcv_ref[0, :, SG_WIDTH:] = _rms(cv, gocv_ref[...]).astype(BF16)


def _mix(x, mod, p, tables):
    bsz, t, d = x.shape
    tm = min(TOKEN_TILE, t)
    halo = BF16_ROWS
    nblk = t // halo
    per = tm // halo
    cos, sn, sp = tables
    const_args = [p["g_mix"], p["w_mix"], p["g_q_lat"], p["w_q"], p["g_kv_lat"], p["w_kv"],
                  p["g_q_slot"], p["g_k_slot"], p["g_k_pe"], p["k_bound"]]
    tail_args = [p["g_sgu"], p["w_sp"], p["b_sp"], p["w_conv"], p["g_out_sg"], p["g_out_cv"]]
    tab_spec = pl.BlockSpec((tm, LANES), lambda b, i: (i, 0))
    in_specs = ([pl.BlockSpec((1, tm, d), lambda b, i: (b, i, 0)),
                 pl.BlockSpec((1, halo, d), lambda b, i: (b, jnp.maximum(i * per - 1, 0), 0)),
                 pl.BlockSpec((1, halo, d), lambda b, i: (b, jnp.minimum((i + 1) * per, nblk - 1), 0)),
                 pl.BlockSpec((1, ADA_CHUNKS, d), lambda b, i: (b, 0, 0))]
                + [_const_spec(a.shape) for a in const_args]
                + [tab_spec, tab_spec, tab_spec]
                + [_const_spec(a.shape) for a in tail_args])
    head_spec = pl.BlockSpec((1, MLA_HEADS, tm, LANES), lambda b, i: (b, 0, i, 0))
    return pl.pallas_call(
        functools.partial(_mix_kernel, tm=tm, q_scale=LOG2E * QK_HEAD_DIM ** -0.5),
        grid=(bsz, t // tm),
        in_specs=in_specs,
        out_specs=[head_spec, head_spec,
                   head_spec,
                   pl.BlockSpec((1, tm, SG_WIDTH + CONV_WIDTH), lambda b, i: (b, i, 0))],
        out_shape=[jax.ShapeDtypeStruct((bsz, MLA_HEADS, t, LANES), BF16),
                   jax.ShapeDtypeStruct((bsz, MLA_HEADS, t, LANES), BF16),
                   jax.ShapeDtypeStruct((bsz, MLA_HEADS, t, LANES), BF16),
                   jax.ShapeDtypeStruct((bsz, t, SG_WIDTH + CONV_WIDTH), BF16)],
        scratch_shapes=[pltpu.VMEM((tm + 2 * halo, d), BF16), pltpu.VMEM((tm + 2 * halo, CONV_WIDTH), F32)],
        compiler_params=pltpu.CompilerParams(vmem_limit_bytes=VMEM_LIMIT_BYTES),
        name="mixer_front",
    )(x, x, x, mod, *const_args, cos, sn, sp, *tail_args)


NEG_BIG = -1e30


def _attn_kernel(*refs, tk, has_extra, online):
    if has_extra:
        q_ref, k_ref, v_ref, k2_ref, v2_ref, o_ref = refs
    else:
        q_ref, k_ref, v_ref, o_ref = refs
    tq = q_ref.shape[2]
    qs = [q_ref[0, h] for h in range(2)]

    def step(qh, kh, vh, state):
        s = lax.dot_general(qh, kh, (((1,), (1,)), ((), ())), preferred_element_type=F32)
        if online:
            m, acc = state
            m_new = jnp.maximum(m, jnp.max(s, axis=-1, keepdims=True))
            acc = jnp.exp2(m - m_new) * acc + _dot(jnp.exp2(s - m_new).astype(BF16), vh)
            return m_new, acc
        (acc,) = state
        return (acc + _dot(jnp.exp2(s).astype(BF16), vh),)

    acc0 = jnp.zeros((tq, LANES), F32)
    init = (jnp.full((tq, 1), NEG_BIG, F32), acc0) if online else (acc0,)

    def body(j, states):
        off = pl.multiple_of(j * tk, tk)
        return tuple(step(qs[h], k_ref[0, h, pl.ds(off, tk), :], v_ref[0, h, pl.ds(off, tk), :], states[h])
                     for h in range(2))

    states = lax.fori_loop(0, k_ref.shape[2] // tk, body, (init, init))
    if has_extra:
        states = tuple(step(qs[h], k2_ref[0, h], v2_ref[0, h], states[h]) for h in range(2))
    outs = []
    for st in states:
        acc = st[-1]
        outs.append(acc / acc[:, V_HEAD_DIM:V_HEAD_DIM + 1])
    lane = lax.broadcasted_iota(jnp.int32, (1, LANES), 1)
    o_ref[0] = jnp.where(lane < V_HEAD_DIM, outs[0], pltpu.roll(outs[1], V_HEAD_DIM, 1))


def _attention(q, k, v, k2=None, v2=None, *, online):
    bsz, heads, t, _ = q.shape
    t1 = k.shape[2]
    tq = min(Q_TILE, t)
    tk = min(KV_TILE, t1)
    has_extra = k2 is not None
    q_spec = pl.BlockSpec((1, 2, tq, LANES), lambda b, p, i: (b, p, i, 0))
    kv_spec = lambda n: pl.BlockSpec((1, 2, n, LANES), lambda b, p, i: (b, p, 0, 0))
    in_specs = [q_spec, kv_spec(t1), kv_spec(t1)]
    args = [q, k, v]
    if has_extra:
        in_specs += [kv_spec(k2.shape[2]), kv_spec(k2.shape[2])]
        args += [k2, v2]
    return pl.pallas_call(
        functools.partial(_attn_kernel, tk=tk, has_extra=has_extra, online=online),
        grid=(bsz, heads // 2, t // tq),
        in_specs=in_specs,
        out_specs=pl.BlockSpec((1, tq, LANES), lambda b, p, i: (b, i, p)),
        out_shape=jax.ShapeDtypeStruct((bsz, t, ATT_WIDTH), F32),
        compiler_params=pltpu.CompilerParams(vmem_limit_bytes=VMEM_LIMIT_BYTES),
        name=("attention_online" if online else "attention_shifted") + ("_extra" if has_extra else ""),
    )(*args)


def _slot_pad(w, width):
    lead = w.shape[:-1]
    w = w.reshape(*lead, MLA_HEADS, width)
    w = jnp.pad(w, [(0, 0)] * len(lead) + [(0, 0), (0, LANES - width)])
    return w.reshape(*lead, MLA_HEADS * LANES)


def _prep_ffn(w_in, w_out):
    d, two_ff = w_in.shape
    d_ff = two_ff // 2
    n = d_ff // FF_CHUNK
    wa = w_in[:, :d_ff].reshape(d, n, FF_CHUNK)
    wb = w_in[:, d_ff:].reshape(d, n, FF_CHUNK)
    wab = jnp.concatenate([wa, wb], axis=-1).transpose(1, 0, 2).astype(BF16)
    return wab, w_out.reshape(n, FF_CHUNK, d).astype(BF16)


def _prep_layer(l, w):
    row = lambda a: a.reshape(1, -1).astype(F32)
    w_mix_in = w["w_mix_in"][l]
    d = w_mix_in.shape[0]
    kpe_cols = jnp.zeros((d, LANES), F32).at[:, NOPE_DIM:QK_HEAD_DIM].set(w_mix_in[:, OFF_KPE:OFF_SG])
    w_mix = jnp.concatenate([w_mix_in[:, :OFF_KPE], w_mix_in[:, OFF_SG:], kpe_cols], axis=1).astype(BF16)
    w_kv = w["w_kv_up"][l].reshape(KV_LORA, MLA_HEADS, NOPE_DIM + V_HEAD_DIM)
    w_k = _slot_pad(w_kv[:, :, :NOPE_DIM].reshape(KV_LORA, -1), NOPE_DIM)
    w_v = _slot_pad(w_kv[:, :, NOPE_DIM:].reshape(KV_LORA, ATT_WIDTH), V_HEAD_DIM)
    g_k = w["g_k_head"][l]
    g_out = w["g_out"][l]
    w_sp = w["w_spatial"][l]
    ffn1 = _prep_ffn(w["w_ffn1_in"][l], w["w_ffn1_out"][l])
    ffn2 = _prep_ffn(w["w_ffn2_in"][l], w["w_ffn2_out"][l])
    return {
        "g_ffn1": row(w["g_ffn1"][l]), "ffn1_ab": ffn1[0], "ffn1_out": ffn1[1],
        "g_ffn2": row(w["g_ffn2"][l]), "ffn2_ab": ffn2[0], "ffn2_out": ffn2[1],
        "g_mix": row(w["g_mix"][l]), "w_mix": w_mix,
        "g_q_lat": row(w["g_q_lat"][l]), "w_q": _slot_pad(w["w_q_up"][l], QK_HEAD_DIM).astype(BF16),
        "g_kv_lat": row(w["g_kv_lat"][l]), "w_kv": jnp.concatenate([w_k, w_v], axis=1).astype(BF16),
        "g_q_slot": row(_slot_pad(jnp.tile(w["g_q_head"][l], MLA_HEADS), QK_HEAD_DIM)),
        "g_k_slot": row(_slot_pad(jnp.tile(g_k[:NOPE_DIM], MLA_HEADS), NOPE_DIM)),
        "g_k_pe": row(jnp.zeros((LANES,), F32).at[NOPE_DIM:QK_HEAD_DIM].set(g_k[NOPE_DIM:])),
        "k_bound": jnp.full((1, LANES), QK_HEAD_DIM ** 0.5, F32) * jnp.max(jnp.abs(g_k)),
        "g_sgu": row(w["g_sgu"][l]),
        "w_sp": jnp.concatenate([w_sp[g] for g in range(SG_GROUPS)], axis=1).astype(BF16),
        "b_sp": jnp.repeat(w["b_spatial"][l].T, SG_GROUP_DIM, axis=1).astype(F32),
        "w_conv": w["w_conv"][l].astype(F32),
        "g_out_att": row(g_out[:ATT_WIDTH]),
        "g_out_sg": row(g_out[ATT_WIDTH:ATT_WIDTH + SG_WIDTH]),
        "g_out_cv": row(g_out[ATT_WIDTH + SG_WIDTH:]),
        "w_mix_out": w["w_mix_out"][l].astype(BF16),
    }


def _rope_tables(n):
    rows = n // GRID_W
    row = jnp.repeat(jnp.arange(rows), GRID_W).astype(F32)
    col = jnp.tile(jnp.arange(GRID_W), rows).astype(F32)
    inv = 1.0 / (ROPE_BASE ** (jnp.arange(0, AXIS_DIM, 2, dtype=F32) / AXIS_DIM))
    ang_r = row[:, None] * inv
    ang_c = col[:, None] * inv
    ang = jnp.concatenate([ang_r, ang_r, ang_c, ang_c], axis=-1)
    cos, sin = jnp.cos(ang), jnp.sin(ang)
    first_half = (jnp.arange(ROPE_DIM) % AXIS_DIM) < AXIS_DIM // 2
    pad = lambda a, fill: jnp.pad(a, ((0, 0), (NOPE_DIM, LANES - QK_HEAD_DIM)), constant_values=fill)
    return (pad(cos, 1.0), pad(jnp.where(first_half, -sin, 0.0), 0.0), pad(jnp.where(first_half, 0.0, sin), 0.0))


def _identity_tables(n):
    return (jnp.ones((n, LANES), F32), jnp.zeros((n, LANES), F32), jnp.zeros((n, LANES), F32))


def kernel(x, c, ctx, c_ctx, w_ada, b_ada, g_ffn1, w_ffn1_in, w_ffn1_out, g_mix, w_mix_in, g_q_lat, w_q_up, g_kv_lat, w_kv_up, g_q_head, g_k_head, g_sgu, w_spatial, b_spatial, w_conv, g_out, w_mix_out, g_ffn2, w_ffn2_in, w_ffn2_out):
    weights = dict(g_ffn1=g_ffn1, w_ffn1_in=w_ffn1_in, w_ffn1_out=w_ffn1_out, g_mix=g_mix, w_mix_in=w_mix_in,
                   g_q_lat=g_q_lat, w_q_up=w_q_up, g_kv_lat=g_kv_lat, w_kv_up=w_kv_up, g_q_head=g_q_head,
                   g_k_head=g_k_head, g_sgu=g_sgu, w_spatial=w_spatial, b_spatial=b_spatial, w_conv=w_conv,
                   g_out=g_out, w_mix_out=w_mix_out, g_ffn2=g_ffn2, w_ffn2_in=w_ffn2_in, w_ffn2_out=w_ffn2_out)
    bsz, t, d = x.shape
    depth = w_ada.shape[0]
    rows = 8 * ((bsz + 1 + 7) // 8)
    cvec = jnp.zeros((rows, d), F32).at[:bsz].set(c).at[bsz].set(c_ctx)
    mod = _ada(cvec, w_ada, b_ada)
    lat_tables = _rope_tables(t)
    ctx_tables = _identity_tables(ctx.shape[1])

    h, hc = x, ctx
    for l in range(depth):
        last = l == depth - 1
        p = _prep_layer(l, weights)
        mod_l = mod[l, :bsz].reshape(bsz, ADA_CHUNKS, d)
        mod_c = jnp.broadcast_to(mod[l, bsz].reshape(1, ADA_CHUNKS, d), (bsz, ADA_CHUNKS, d))

        h = _ffn(h, mod_l, p["g_ffn1"], p["ffn1_ab"], p["ffn1_out"], mod_base=0)
        hc = _ffn(hc, mod_c, p["g_ffn1"], p["ffn1_ab"], p["ffn1_out"], mod_base=0)

        q, k, v, sgcv = _mix(h, mod_l, p, lat_tables)
        qc, kc, vc, sgcv_c = _mix(hc, mod_c, p, ctx_tables)
        max_shift = (LOG2E * QK_HEAD_DIM ** 0.5) * jnp.max(jnp.abs(g_q_head[l])) * jnp.max(jnp.abs(g_k_head[l]))
        attn = lax.cond(max_shift <= MAX_SAFE_SHIFT,
                        functools.partial(_attention, online=False),
                        functools.partial(_attention, online=True), q, k, v, kc, vc)
        h = _ffn(h, mod_l, p["g_ffn2"], p["ffn2_ab"], p["ffn2_out"], mod_base=6,
                 merge_args=(attn, sgcv, p["g_out_att"], p["w_mix_out"]))
        if not last:
            attn_c = _attention(qc, kc, vc, online=True)
            hc = _ffn(hc, mod_c, p["g_ffn2"], p["ffn2_ab"], p["ffn2_out"], mod_base=6,
                      merge_args=(attn_c, sgcv_c, p["g_out_att"], p["w_mix_out"]))
    return h
```

```python
import functools

import jax
import jax.numpy as jnp
from jax import lax
from jax.experimental import pallas as pl
from jax.experimental.pallas import tpu as pltpu

EPS = 1e-6
ADA_CHUNKS = 9
FFN_RESIDUAL_WEIGHT = 0.5
GRID_W = 64
ROPE_BASE = 10000.0
MLA_HEADS = 8
NOPE_DIM = 64
ROPE_DIM = 32
AXIS_DIM = ROPE_DIM // 2
QK_HEAD_DIM = NOPE_DIM + ROPE_DIM
V_HEAD_DIM = 64
Q_LORA = 384
KV_LORA = 256
ATT_WIDTH = MLA_HEADS * V_HEAD_DIM
SG_GROUPS = 4
SG_GROUP_DIM = 64
SG_WIDTH = SG_GROUPS * SG_GROUP_DIM
CHUNK = 128
CONV_WIDTH = 256
OFF_KV = Q_LORA
OFF_KPE = OFF_KV + KV_LORA
OFF_SG = OFF_KPE + ROPE_DIM
OFF_CONV = OFF_SG + 2 * SG_WIDTH
LOG2E = 1.4426950408889634
SHIFT_LANE = QK_HEAD_DIM
MAX_SAFE_SHIFT = 60.0

LANES = 128
BF16_ROWS = 16
VMEM_LIMIT_BYTES = 56 * 1024 * 1024

MIX_Q = 0
MIX_KV = MIX_Q + Q_LORA
MIX_SG = MIX_KV + KV_LORA
MIX_CV = MIX_SG + 2 * SG_WIDTH
MIX_KPE = MIX_CV + 3 * CONV_WIDTH
MIX_COLS = MIX_KPE + LANES

FF_CHUNK = 256
TOKEN_TILE = 512
Q_TILE = 512
KV_TILE = 2048

BF16 = jnp.bfloat16
F32 = jnp.float32


def _dot(a, b):
    return jnp.dot(a, b, preferred_element_type=F32)


def _rms(x, g):
    return x * lax.rsqrt(jnp.mean(x * x, axis=-1, keepdims=True) + EPS) * g


def _silu(x):
    return x * jax.nn.sigmoid(x)


def _gelu_tanh(x):
    return 0.5 * x * (1.0 + jnp.tanh(0.7978845608028654 * (x + 0.044715 * (x * x * x))))


def _const_spec(shape):
    zeros = (0,) * len(shape)
    return pl.BlockSpec(shape, lambda *_: zeros, pipeline_mode=pl.Buffered(1))


def _ada_kernel(c_ref, w_ref, b_ref, o_ref):
    s = _silu(c_ref[...])
    w = w_ref[0]
    s_hi = s.astype(BF16)
    s_lo = (s - s_hi.astype(F32)).astype(BF16)
    w_hi = w.astype(BF16)
    w_lo = (w - w_hi.astype(F32)).astype(BF16)
    o_ref[0] = _dot(s_hi, w_hi) + _dot(s_lo, w_hi) + _dot(s_hi, w_lo) + b_ref[0]


def _ada(cvec, w_ada, b_ada):
    depth, d, n = w_ada.shape
    rows = cvec.shape[0]
    tn = d
    return pl.pallas_call(
        _ada_kernel,
        grid=(depth, n // tn),
        in_specs=[
            pl.BlockSpec((rows, d), lambda l, j: (0, 0)),
            pl.BlockSpec((1, d, tn), lambda l, j: (l, 0, j)),
            pl.BlockSpec((1, 1, tn), lambda l, j: (l, 0, j)),
        ],
        out_specs=pl.BlockSpec((1, rows, tn), lambda l, j: (l, 0, j)),
        out_shape=jax.ShapeDtypeStruct((depth, rows, n), F32),
        compiler_params=pltpu.CompilerParams(vmem_limit_bytes=VMEM_LIMIT_BYTES),
        name="ada_modulation",
    )(cvec, w_ada, b_ada.reshape(depth, 1, n))


def _ffn_kernel(*refs, merge, mod_base, n_chunks):
    if merge:
        (x_ref, mod_ref, attn_ref, sgcv_ref, gatt_ref, wmo_ref, g_ref, wab_ref, wout_ref,
         o_ref, h_scr, acc_scr) = refs
    else:
        x_ref, mod_ref, g_ref, wab_ref, wout_ref, o_ref, h_scr, acc_scr = refs
    x = x_ref[0]
    if merge:
        att = _rms(attn_ref[0], gatt_ref[...]).astype(BF16)
        y = _dot(att, wmo_ref[0:ATT_WIDTH, :]) + _dot(sgcv_ref[0], wmo_ref[ATT_WIDTH:, :])
        x = x + mod_ref[0, 5:6, :] * y
    o_ref[0] = x
    shift = mod_ref[0, mod_base:mod_base + 1, :]
    scale = mod_ref[0, mod_base + 1:mod_base + 2, :]
    h_scr[...] = (_rms(x, g_ref[...]) * (1.0 + scale) + shift).astype(BF16)

    def chunk(c):
        ab = _dot(h_scr[...], wab_ref[c])
        a = ab[:, :FF_CHUNK]
        b = ab[:, FF_CHUNK:]
        return _dot((_silu(a) * b).astype(BF16), wout_ref[c])

    acc_scr[...] = chunk(0)

    def body(c, carry):
        acc_scr[...] += chunk(c)
        return carry

    lax.fori_loop(1, n_chunks, body, 0, unroll=True)
    gate = mod_ref[0, mod_base + 2:mod_base + 3, :]
    o_ref[0] = o_ref[0] + (FFN_RESIDUAL_WEIGHT * gate) * acc_scr[...]


def _ffn(x, mod, g, wab, wout, *, mod_base, merge_args=None):
    bsz, t, d = x.shape
    tm = min(TOKEN_TILE, t)
    n_chunks = wab.shape[0]
    tile = lambda w: pl.BlockSpec((1, tm, w), lambda b, i: (b, i, 0))
    in_specs = [tile(d), pl.BlockSpec((1, ADA_CHUNKS, d), lambda b, i: (b, 0, 0))]
    args = [x, mod]
    if merge_args is not None:
        attn, sgcv, g_att, wmo = merge_args
        in_specs += [tile(ATT_WIDTH), tile(SG_WIDTH + CONV_WIDTH), _const_spec(g_att.shape), _const_spec(wmo.shape)]
        args += [attn, sgcv, g_att, wmo]
    in_specs += [_const_spec(g.shape), _const_spec(wab.shape), _const_spec(wout.shape)]
    args += [g, wab, wout]
    return pl.pallas_call(
        functools.partial(_ffn_kernel, merge=merge_args is not None, mod_base=mod_base, n_chunks=n_chunks),
        grid=(bsz, t // tm),
        in_specs=in_specs,
        out_specs=tile(d),
        out_shape=jax.ShapeDtypeStruct(x.shape, F32),
        scratch_shapes=[pltpu.VMEM((tm, d), BF16), pltpu.VMEM((tm, d), F32)],
        compiler_params=pltpu.CompilerParams(vmem_limit_bytes=VMEM_LIMIT_BYTES),
        name="ffn_merge" if merge_args is not None else "ffn",
    )(*args)


def _rope(xg, cos, sin_next, sin_prev):
    return xg * cos + pltpu.roll(xg, LANES - 8, 1) * sin_next + pltpu.roll(xg, 8, 1) * sin_prev


def _mix_kernel(x_ref, xp_ref, xn_ref, mod_ref, gmix_ref, w_ref, gql_ref, wq_ref, gkvl_ref, wkv_ref,
                gqs_ref, gks_ref, gkpe_ref, shift_ref, cos_ref, sn_ref, sp_ref, gsgu_ref, wsp_ref, bsp_ref,
                wcv_ref, gosg_ref, gocv_ref, q_ref, k_ref, v_ref, sgcv_ref, h_scr, z_scr, *, tm, q_scale):
    i = pl.program_id(1)
    last = pl.num_programs(1) - 1
    halo = BF16_ROWS
    shift = mod_ref[0, 3:4, :]
    scale1 = 1.0 + mod_ref[0, 4:5, :]
    gmix = gmix_ref[...]

    def norm_mod(x):
        return (_rms(x, gmix) * scale1 + shift).astype(BF16)

    h_scr[0:halo] = norm_mod(xp_ref[0])
    h_scr[halo:halo + tm] = norm_mod(x_ref[0])
    h_scr[halo + tm:] = norm_mod(xn_ref[0])
    hm = h_scr[halo:halo + tm]

    cos = cos_ref[...]
    sn = sn_ref[...]
    sp = sp_ref[...]

    qn = _rms(_dot(hm, w_ref[:, MIX_Q:MIX_Q + Q_LORA]), gql_ref[...]).astype(BF16)
    qs = _dot(qn, wq_ref[...])
    lane = lax.broadcasted_iota(jnp.int32, (1, LANES), 1)
    for h in range(MLA_HEADS):
        sl = slice(h * LANES, (h + 1) * LANES)
        xq = qs[:, sl]
        r = lax.rsqrt(jnp.sum(xq * xq, axis=-1, keepdims=True) * (1.0 / QK_HEAD_DIM) + EPS)
        y = _rope(xq * gqs_ref[:, sl], cos, sn, sp) * (r * q_scale)
        q_ref[0, h] = (y + shift_ref[...]).astype(BF16)

    kvn = _rms(_dot(hm, w_ref[:, MIX_KV:MIX_KV + KV_LORA]), gkvl_ref[...]).astype(BF16)
    kv = _dot(kvn, wkv_ref[...])
    kpe = _dot(hm, w_ref[:, MIX_KPE:MIX_KPE + LANES])
    ss_pe = jnp.sum(kpe * kpe, axis=-1, keepdims=True)
    pe = _rope(kpe * gkpe_ref[...], cos, sn, sp)
    k_one = jnp.where(lane == SHIFT_LANE, 1.0, 0.0)
    v_one = jnp.where(lane == V_HEAD_DIM, 1.0, 0.0)
    for h in range(MLA_HEADS):
        sl = slice(h * LANES, (h + 1) * LANES)
        xk = kv[:, sl]
        r = lax.rsqrt((jnp.sum(xk * xk, axis=-1, keepdims=True) + ss_pe) * (1.0 / QK_HEAD_DIM) + EPS)
        k_ref[0, h] = ((xk * gks_ref[:, sl] + pe) * r + k_one).astype(BF16)
        xv = kv[:, (MLA_HEADS + h) * LANES:(MLA_HEADS + h + 1) * LANES]
        v_ref[0, h] = (xv + v_one).astype(BF16)

    ge = _gelu_tanh(_dot(hm, w_ref[:, MIX_SG:MIX_SG + 2 * SG_WIDTH]))
    u = ge[:, :SG_WIDTH]
    low = lane < SG_GROUP_DIM
    vparts = []
    for c in range(SG_WIDTH // LANES):
        vv = ge[:, SG_WIDTH + c * LANES:SG_WIDTH + (c + 1) * LANES]
        v2 = vv * vv
        s_lo = jnp.sum(jnp.where(low, v2, 0.0), axis=-1, keepdims=True)
        s_hi = jnp.sum(jnp.where(low, 0.0, v2), axis=-1, keepdims=True)
        r = jnp.where(low, lax.rsqrt(s_lo * (1.0 / SG_GROUP_DIM) + EPS), lax.rsqrt(s_hi * (1.0 / SG_GROUP_DIM) + EPS))
        vparts.append(vv * r * gsgu_ref[:, c * LANES:(c + 1) * LANES])
    vn = jnp.concatenate(vparts, axis=-1)
    group = lax.broadcasted_iota(jnp.int32, (1, SG_WIDTH), 1) // SG_GROUP_DIM
    sg_rows = []
    for c in range(tm // CHUNK):
        vc = vn[c * CHUNK:(c + 1) * CHUNK]
        stacked = jnp.concatenate([jnp.where(group == g, vc, 0.0) for g in range(SG_GROUPS)], axis=0)
        vs = _dot(wsp_ref[...], stacked.astype(BF16)) + bsp_ref[...]
        sg_rows.append(u[c * CHUNK:(c + 1) * CHUNK] * vs)
    sg = jnp.concatenate(sg_rows, axis=0)
    sgcv_ref[0, :, 0:SG_WIDTH] = _rms(sg, gosg_ref[...]).astype(BF16)

    cvin = _dot(h_scr[...], w_ref[:, MIX_CV:MIX_CV + 3 * CONV_WIDTH])
    z = cvin[:, CONV_WIDTH:2 * CONV_WIDTH] * cvin[:, 2 * CONV_WIDTH:]
    z_scr[...] = z
    z_scr[0:halo] = jnp.where(i > 0, z[0:halo], 0.0)
    z_scr[halo + tm:] = jnp.where(i < last, z[halo + tm:], 0.0)
    y = (wcv_ref[0:1, :] * z_scr[halo - 1:halo - 1 + tm] + wcv_ref[1:2, :] * z_scr[halo:halo + tm]
         + wcv_ref[2:3, :] * z_scr[halo + 1:halo + 1 + tm])
    cv = cvin[halo:halo + tm, 0:CONV_WIDTH] * y
    sgcv_ref[0, :, SG_WIDTH:] = _rms(cv, gocv_ref[...]).astype(BF16)


def _mix(x, mod, p, tables):
    bsz, t, d = x.shape
    tm = min(TOKEN_TILE, t)
    halo = BF16_ROWS
    nblk = t // halo
    per = tm // halo
    cos, sn, sp = tables
    const_args = [p["g_mix"], p["w_mix"], p["g_q_lat"], p["w_q"], p["g_kv_lat"], p["w_kv"],
                  p["g_q_slot"], p["g_k_slot"], p["g_k_pe"], p["q_shift"]]
    tail_args = [p["g_sgu"], p["w_sp"], p["b_sp"], p["w_conv"], p["g_out_sg"], p["g_out_cv"]]
    tab_spec = pl.BlockSpec((tm, LANES), lambda b, i: (i, 0))
    in_specs = ([pl.BlockSpec((1, tm, d), lambda b, i: (b, i, 0)),
                 pl.BlockSpec((1, halo, d), lambda b, i: (b, jnp.maximum(i * per - 1, 0), 0)),
                 pl.BlockSpec((1, halo, d), lambda b, i: (b, jnp.minimum((i + 1) * per, nblk - 1), 0)),
                 pl.BlockSpec((1, ADA_CHUNKS, d), lambda b, i: (b, 0, 0))]
                + [_const_spec(a.shape) for a in const_args]
                + [tab_spec, tab_spec, tab_spec]
                + [_const_spec(a.shape) for a in tail_args])
    head_spec = pl.BlockSpec((1, MLA_HEADS, tm, LANES), lambda b, i: (b, 0, i, 0))
    return pl.pallas_call(
        functools.partial(_mix_kernel, tm=tm, q_scale=LOG2E * QK_HEAD_DIM ** -0.5),
        grid=(bsz, t // tm),
        in_specs=in_specs,
        out_specs=[head_spec, head_spec,
                   head_spec,
                   pl.BlockSpec((1, tm, SG_WIDTH + CONV_WIDTH), lambda b, i: (b, i, 0))],
        out_shape=[jax.ShapeDtypeStruct((bsz, MLA_HEADS, t, LANES), BF16),
                   jax.ShapeDtypeStruct((bsz, MLA_HEADS, t, LANES), BF16),
                   jax.ShapeDtypeStruct((bsz, MLA_HEADS, t, LANES), BF16),
                   jax.ShapeDtypeStruct((bsz, t, SG_WIDTH + CONV_WIDTH), BF16)],
        scratch_shapes=[pltpu.VMEM((tm + 2 * halo, d), BF16), pltpu.VMEM((tm + 2 * halo, CONV_WIDTH), F32)],
        compiler_params=pltpu.CompilerParams(vmem_limit_bytes=VMEM_LIMIT_BYTES),
        name="mixer_front",
    )(x, x, x, mod, *const_args, cos, sn, sp, *tail_args)


NEG_BIG = -1e30


def _attn_kernel(*refs, tk, has_extra, online):
    if has_extra:
        q_ref, k_ref, v_ref, k2_ref, v2_ref, o_ref = refs
    else:
        q_ref, k_ref, v_ref, o_ref = refs
    tq = q_ref.shape[2]
    qs = [q_ref[0, h] for h in range(2)]

    def step(qh, kh, vh, state):
        s = lax.dot_general(qh, kh, (((1,), (1,)), ((), ())), preferred_element_type=F32)
        if online:
            m, acc = state
            m_new = jnp.maximum(m, jnp.max(s, axis=-1, keepdims=True))
            acc = jnp.exp2(m - m_new) * acc + _dot(jnp.exp2(s - m_new).astype(BF16), vh)
            return m_new, acc
        (acc,) = state
        return (acc + _dot(jnp.exp2(s).astype(BF16), vh),)

    acc0 = jnp.zeros((tq, LANES), F32)
    init = (jnp.full((tq, 1), NEG_BIG, F32), acc0) if online else (acc0,)

    def body(j, states):
        off = pl.multiple_of(j * tk, tk)
        return tuple(step(qs[h], k_ref[0, h, pl.ds(off, tk), :], v_ref[0, h, pl.ds(off, tk), :], states[h])
                     for h in range(2))

    states = lax.fori_loop(0, k_ref.shape[2] // tk, body, (init, init), unroll=True)
    if has_extra:
        states = tuple(step(qs[h], k2_ref[0, h], v2_ref[0, h], states[h]) for h in range(2))
    outs = []
    for st in states:
        acc = st[-1]
        outs.append(acc / acc[:, V_HEAD_DIM:V_HEAD_DIM + 1])
    lane = lax.broadcasted_iota(jnp.int32, (1, LANES), 1)
    o_ref[0] = jnp.where(lane < V_HEAD_DIM, outs[0], pltpu.roll(outs[1], V_HEAD_DIM, 1))


def _attention(q, k, v, k2=None, v2=None, *, online):
    bsz, heads, t, _ = q.shape
    t1 = k.shape[2]
    tq = min(Q_TILE, t)
    tk = min(KV_TILE, t1)
    has_extra = k2 is not None
    q_spec = pl.BlockSpec((1, 2, tq, LANES), lambda b, p, i: (b, p, i, 0))
    kv_spec = lambda n: pl.BlockSpec((1, 2, n, LANES), lambda b, p, i: (b, p, 0, 0))
    in_specs = [q_spec, kv_spec(t1), kv_spec(t1)]
    args = [q, k, v]
    if has_extra:
        in_specs += [kv_spec(k2.shape[2]), kv_spec(k2.shape[2])]
        args += [k2, v2]
    return pl.pallas_call(
        functools.partial(_attn_kernel, tk=tk, has_extra=has_extra, online=online),
        grid=(bsz, heads // 2, t // tq),
        in_specs=in_specs,
        out_specs=pl.BlockSpec((1, tq, LANES), lambda b, p, i: (b, i, p)),
        out_shape=jax.ShapeDtypeStruct((bsz, t, ATT_WIDTH), F32),
        compiler_params=pltpu.CompilerParams(vmem_limit_bytes=VMEM_LIMIT_BYTES),
        name=("attention_online" if online else "attention_shifted") + ("_extra" if has_extra else ""),
    )(*args)


def _slot_pad(w, width):
    lead = w.shape[:-1]
    w = w.reshape(*lead, MLA_HEADS, width)
    w = jnp.pad(w, [(0, 0)] * len(lead) + [(0, 0), (0, LANES - width)])
    return w.reshape(*lead, MLA_HEADS * LANES)


def _prep_ffn(w_in, w_out):
    d, two_ff = w_in.shape
    d_ff = two_ff // 2
    n = d_ff // FF_CHUNK
    wa = w_in[:, :d_ff].reshape(d, n, FF_CHUNK)
    wb = w_in[:, d_ff:].reshape(d, n, FF_CHUNK)
    wab = jnp.concatenate([wa, wb], axis=-1).transpose(1, 0, 2).astype(BF16)
    return wab, w_out.reshape(n, FF_CHUNK, d).astype(BF16)


def _max_shift(g_q, g_k):
    return (LOG2E * QK_HEAD_DIM ** 0.5) * jnp.max(jnp.abs(g_q)) * jnp.max(jnp.abs(g_k))


def _prep_layer(l, w):
    row = lambda a: a.reshape(1, -1).astype(F32)
    w_mix_in = w["w_mix_in"][l]
    d = w_mix_in.shape[0]
    kpe_cols = jnp.zeros((d, LANES), F32).at[:, NOPE_DIM:QK_HEAD_DIM].set(w_mix_in[:, OFF_KPE:OFF_SG])
    w_mix = jnp.concatenate([w_mix_in[:, :OFF_KPE], w_mix_in[:, OFF_SG:], kpe_cols], axis=1).astype(BF16)
    w_kv = w["w_kv_up"][l].reshape(KV_LORA, MLA_HEADS, NOPE_DIM + V_HEAD_DIM)
    w_k = _slot_pad(w_kv[:, :, :NOPE_DIM].reshape(KV_LORA, -1), NOPE_DIM)
    w_v = _slot_pad(w_kv[:, :, NOPE_DIM:].reshape(KV_LORA, ATT_WIDTH), V_HEAD_DIM)
    g_k = w["g_k_head"][l]
    g_out = w["g_out"][l]
    w_sp = w["w_spatial"][l]
    ffn1 = _prep_ffn(w["w_ffn1_in"][l], w["w_ffn1_out"][l])
    ffn2 = _prep_ffn(w["w_ffn2_in"][l], w["w_ffn2_out"][l])
    return {
        "g_ffn1": row(w["g_ffn1"][l]), "ffn1_ab": ffn1[0], "ffn1_out": ffn1[1],
        "g_ffn2": row(w["g_ffn2"][l]), "ffn2_ab": ffn2[0], "ffn2_out": ffn2[1],
        "g_mix": row(w["g_mix"][l]), "w_mix": w_mix,
        "g_q_lat": row(w["g_q_lat"][l]), "w_q": _slot_pad(w["w_q_up"][l], QK_HEAD_DIM).astype(BF16),
        "g_kv_lat": row(w["g_kv_lat"][l]), "w_kv": jnp.concatenate([w_k, w_v], axis=1).astype(BF16),
        "g_q_slot": row(_slot_pad(jnp.tile(w["g_q_head"][l], MLA_HEADS), QK_HEAD_DIM)),
        "g_k_slot": row(_slot_pad(jnp.tile(g_k[:NOPE_DIM], MLA_HEADS), NOPE_DIM)),
        "g_k_pe": row(jnp.zeros((LANES,), F32).at[NOPE_DIM:QK_HEAD_DIM].set(g_k[NOPE_DIM:])),
        "q_shift": jnp.zeros((1, LANES), F32).at[0, SHIFT_LANE].set(-_max_shift(w["g_q_head"][l], g_k)),
        "g_sgu": row(w["g_sgu"][l]),
        "w_sp": jnp.concatenate([w_sp[g] for g in range(SG_GROUPS)], axis=1).astype(BF16),
        "b_sp": jnp.repeat(w["b_spatial"][l].T, SG_GROUP_DIM, axis=1).astype(F32),
        "w_conv": w["w_conv"][l].astype(F32),
        "g_out_att": row(g_out[:ATT_WIDTH]),
        "g_out_sg": row(g_out[ATT_WIDTH:ATT_WIDTH + SG_WIDTH]),
        "g_out_cv": row(g_out[ATT_WIDTH + SG_WIDTH:]),
        "w_mix_out": w["w_mix_out"][l].astype(BF16),
    }


def _rope_tables(n):
    rows = n // GRID_W
    row = jnp.repeat(jnp.arange(rows), GRID_W).astype(F32)
    col = jnp.tile(jnp.arange(GRID_W), rows).astype(F32)
    inv = 1.0 / (ROPE_BASE ** (jnp.arange(0, AXIS_DIM, 2, dtype=F32) / AXIS_DIM))
    ang_r = row[:, None] * inv
    ang_c = col[:, None] * inv
    ang = jnp.concatenate([ang_r, ang_r, ang_c, ang_c], axis=-1)
    cos, sin = jnp.cos(ang), jnp.sin(ang)
    first_half = (jnp.arange(ROPE_DIM) % AXIS_DIM) < AXIS_DIM // 2
    pad = lambda a, fill: jnp.pad(a, ((0, 0), (NOPE_DIM, LANES - QK_HEAD_DIM)), constant_values=fill)
    return (pad(cos, 1.0), pad(jnp.where(first_half, -sin, 0.0), 0.0), pad(jnp.where(first_half, 0.0, sin), 0.0))


def _identity_tables(n):
    return (jnp.ones((n, LANES), F32), jnp.zeros((n, LANES), F32), jnp.zeros((n, LANES), F32))


def kernel(x, c, ctx, c_ctx, w_ada, b_ada, g_ffn1, w_ffn1_in, w_ffn1_out, g_mix, w_mix_in, g_q_lat, w_q_up, g_kv_lat, w_kv_up, g_q_head, g_k_head, g_sgu, w_spatial, b_spatial, w_conv, g_out, w_mix_out, g_ffn2, w_ffn2_in, w_ffn2_out):
    weights = dict(g_ffn1=g_ffn1, w_ffn1_in=w_ffn1_in, w_ffn1_out=w_ffn1_out, g_mix=g_mix, w_mix_in=w_mix_in,
                   g_q_lat=g_q_lat, w_q_up=w_q_up, g_kv_lat=g_kv_lat, w_kv_up=w_kv_up, g_q_head=g_q_head,
                   g_k_head=g_k_head, g_sgu=g_sgu, w_spatial=w_spatial, b_spatial=b_spatial, w_conv=w_conv,
                   g_out=g_out, w_mix_out=w_mix_out, g_ffn2=g_ffn2, w_ffn2_in=w_ffn2_in, w_ffn2_out=w_ffn2_out)
    bsz, t, d = x.shape
    depth = w_ada.shape[0]
    rows = 8 * ((bsz + 1 + 7) // 8)
    cvec = jnp.zeros((rows, d), F32).at[:bsz].set(c).at[bsz].set(c_ctx)
    mod = _ada(cvec, w_ada, b_ada)
    lat_tables = _rope_tables(t)
    ctx_tables = _identity_tables(ctx.shape[1])

    h, hc = x, ctx
    for l in range(depth):
        last = l == depth - 1
        p = _prep_layer(l, weights)
        mod_l = mod[l, :bsz].reshape(bsz, ADA_CHUNKS, d)
        mod_c = jnp.broadcast_to(mod[l, bsz].reshape(1, ADA_CHUNKS, d), (bsz, ADA_CHUNKS, d))

        h = _ffn(h, mod_l, p["g_ffn1"], p["ffn1_ab"], p["ffn1_out"], mod_base=0)
        hc = _ffn(hc, mod_c, p["g_ffn1"], p["ffn1_ab"], p["ffn1_out"], mod_base=0)

        q, k, v, sgcv = _mix(h, mod_l, p, lat_tables)
        qc, kc, vc, sgcv_c = _mix(hc, mod_c, p, ctx_tables)
        attn = lax.cond(_max_shift(g_q_head[l], g_k_head[l]) <= MAX_SAFE_SHIFT,
                        functools.partial(_attention, online=False),
                        functools.partial(_attention, online=True), q, k, v, kc, vc)
        h = _ffn(h, mod_l, p["g_ffn2"], p["ffn2_ab"], p["ffn2_out"], mod_base=6,
                 merge_args=(attn, sgcv, p["g_out_att"], p["w_mix_out"]))
        if not last:
            attn_c = _attention(qc, kc, vc, online=True)
            hc = _ffn(hc, mod_c, p["g_ffn2"], p["ffn2_ab"], p["ffn2_out"], mod_base=6,
                      merge_args=(attn_c, sgcv_c, p["g_out_att"], p["w_mix_out"]))
    return h
```

```python
import functools

import jax
import jax.numpy as jnp
from jax import lax
from jax.experimental import pallas as pl
from jax.experimental.pallas import tpu as pltpu

EPS = 1e-6
ADA_CHUNKS = 9
FFN_RESIDUAL_WEIGHT = 0.5
GRID_W = 64
ROPE_BASE = 10000.0
MLA_HEADS = 8
NOPE_DIM = 64
ROPE_DIM = 32
AXIS_DIM = ROPE_DIM // 2
QK_HEAD_DIM = NOPE_DIM + ROPE_DIM
V_HEAD_DIM = 64
Q_LORA = 384
KV_LORA = 256
ATT_WIDTH = MLA_HEADS * V_HEAD_DIM
SG_GROUPS = 4
SG_GROUP_DIM = 64
SG_WIDTH = SG_GROUPS * SG_GROUP_DIM
CHUNK = 128
CONV_WIDTH = 256
OFF_KV = Q_LORA
OFF_KPE = OFF_KV + KV_LORA
OFF_SG = OFF_KPE + ROPE_DIM
OFF_CONV = OFF_SG + 2 * SG_WIDTH
LOG2E = 1.4426950408889634
SHIFT_LANE = QK_HEAD_DIM
MAX_SAFE_SHIFT = 60.0

LANES = 128
BF16_ROWS = 16
VMEM_LIMIT_BYTES = 56 * 1024 * 1024

MIX_Q = 0
MIX_KV = MIX_Q + Q_LORA
MIX_SG = MIX_KV + KV_LORA
MIX_CV = MIX_SG + 2 * SG_WIDTH
MIX_KPE = MIX_CV + 3 * CONV_WIDTH
MIX_COLS = MIX_KPE + LANES

FF_CHUNK = 256
TOKEN_TILE = 512
Q_TILE = 512
KV_TILE = 2048

BF16 = jnp.bfloat16
F32 = jnp.float32


def _dot(a, b):
    return jnp.dot(a, b, preferred_element_type=F32)


def _rms(x, g):
    return x * lax.rsqrt(jnp.mean(x * x, axis=-1, keepdims=True) + EPS) * g


def _silu(x):
    return x * jax.nn.sigmoid(x)


def _gelu_tanh(x):
    return 0.5 * x * (1.0 + jnp.tanh(0.7978845608028654 * (x + 0.044715 * (x * x * x))))


def _const_spec(shape):
    zeros = (0,) * len(shape)
    return pl.BlockSpec(shape, lambda *_: zeros, pipeline_mode=pl.Buffered(1))


def _ada_kernel(c_ref, w_ref, b_ref, o_ref):
    s = _silu(c_ref[...])
    w = w_ref[0]
    s_hi = s.astype(BF16)
    s_lo = (s - s_hi.astype(F32)).astype(BF16)
    w_hi = w.astype(BF16)
    w_lo = (w - w_hi.astype(F32)).astype(BF16)
    o_ref[0] = _dot(s_hi, w_hi) + _dot(s_lo, w_hi) + _dot(s_hi, w_lo) + b_ref[0]


def _ada(cvec, w_ada, b_ada):
    depth, d, n = w_ada.shape
    rows = cvec.shape[0]
    tn = d
    return pl.pallas_call(
        _ada_kernel,
        grid=(depth, n // tn),
        in_specs=[
            pl.BlockSpec((rows, d), lambda l, j: (0, 0)),
            pl.BlockSpec((1, d, tn), lambda l, j: (l, 0, j)),
            pl.BlockSpec((1, 1, tn), lambda l, j: (l, 0, j)),
        ],
        out_specs=pl.BlockSpec((1, rows, tn), lambda l, j: (l, 0, j)),
        out_shape=jax.ShapeDtypeStruct((depth, rows, n), F32),
        compiler_params=pltpu.CompilerParams(vmem_limit_bytes=VMEM_LIMIT_BYTES),
        name="ada_modulation",
    )(cvec, w_ada, b_ada.reshape(depth, 1, n))


def _ffn_kernel(*refs, merge, mod_base, n_chunks):
    if merge:
        (x_ref, mod_ref, attn_ref, sgcv_ref, gatt_ref, wmo_ref, g_ref, wab_ref, wout_ref,
         o_ref, h_scr, acc_scr) = refs
    else:
        x_ref, mod_ref, g_ref, wab_ref, wout_ref, o_ref, h_scr, acc_scr = refs
    x = x_ref[0]
    if merge:
        att = _rms(attn_ref[0], gatt_ref[...]).astype(BF16)
        y = _dot(att, wmo_ref[0:ATT_WIDTH, :]) + _dot(sgcv_ref[0], wmo_ref[ATT_WIDTH:, :])
        x = x + mod_ref[0, 5:6, :] * y
    o_ref[0] = x
    shift = mod_ref[0, mod_base:mod_base + 1, :]
    scale = mod_ref[0, mod_base + 1:mod_base + 2, :]
    h_scr[...] = (_rms(x, g_ref[...]) * (1.0 + scale) + shift).astype(BF16)

    def chunk(c):
        ab = _dot(h_scr[...], wab_ref[c])
        a = ab[:, :FF_CHUNK]
        b = ab[:, FF_CHUNK:]
        return _dot((_silu(a) * b).astype(BF16), wout_ref[c])

    acc_scr[...] = chunk(0)

    def body(c, carry):
        acc_scr[...] += chunk(c)
        return carry

    lax.fori_loop(1, n_chunks, body, 0, unroll=True)
    gate = mod_ref[0, mod_base + 2:mod_base + 3, :]
    o_ref[0] = o_ref[0] + (FFN_RESIDUAL_WEIGHT * gate) * acc_scr[...]


def _ffn(x, mod, g, wab, wout, *, mod_base, merge_args=None):
    bsz, t, d = x.shape
    tm = min(TOKEN_TILE, t)
    n_chunks = wab.shape[0]
    tile = lambda w: pl.BlockSpec((1, tm, w), lambda b, i: (b, i, 0))
    in_specs = [tile(d), pl.BlockSpec((1, ADA_CHUNKS, d), lambda b, i: (b, 0, 0))]
    args = [x, mod]
    if merge_args is not None:
        attn, sgcv, g_att, wmo = merge_args
        in_specs += [tile(ATT_WIDTH), tile(SG_WIDTH + CONV_WIDTH), _const_spec(g_att.shape), _const_spec(wmo.shape)]
        args += [attn, sgcv, g_att, wmo]
    in_specs += [_const_spec(g.shape), _const_spec(wab.shape), _const_spec(wout.shape)]
    args += [g, wab, wout]
    return pl.pallas_call(
        functools.partial(_ffn_kernel, merge=merge_args is not None, mod_base=mod_base, n_chunks=n_chunks),
        grid=(bsz, t // tm),
        in_specs=in_specs,
        out_specs=tile(d),
        out_shape=jax.ShapeDtypeStruct(x.shape, F32),
        scratch_shapes=[pltpu.VMEM((tm, d), BF16), pltpu.VMEM((tm, d), F32)],
        compiler_params=pltpu.CompilerParams(vmem_limit_bytes=VMEM_LIMIT_BYTES),
        name="ffn_merge" if merge_args is not None else "ffn",
    )(*args)


def _rope(xg, cos, sin_next, sin_prev):
    return xg * cos + pltpu.roll(xg, LANES - 8, 1) * sin_next + pltpu.roll(xg, 8, 1) * sin_prev


def _mix_kernel(x_ref, xp_ref, xn_ref, mod_ref, gmix_ref, w_ref, gql_ref, wq_ref, gkvl_ref, wk_ref, wvt_ref,
                gqs_ref, gks_ref, gkpe_ref, shift_ref, cos_ref, sn_ref, sp_ref, gsgu_ref, wsp_ref, bsp_ref,
                wcv_ref, gosg_ref, gocv_ref, q_ref, k_ref, v_ref, sgcv_ref, h_scr, z_scr, *, tm, q_scale):
    i = pl.program_id(1)
    last = pl.num_programs(1) - 1
    halo = BF16_ROWS
    shift = mod_ref[0, 3:4, :]
    scale1 = 1.0 + mod_ref[0, 4:5, :]
    gmix = gmix_ref[...]

    def norm_mod(x):
        return (_rms(x, gmix) * scale1 + shift).astype(BF16)

    h_scr[0:halo] = norm_mod(xp_ref[0])
    h_scr[halo:halo + tm] = norm_mod(x_ref[0])
    h_scr[halo + tm:] = norm_mod(xn_ref[0])
    hm = h_scr[halo:halo + tm]

    cos = cos_ref[...]
    sn = sn_ref[...]
    sp = sp_ref[...]

    qn = _rms(_dot(hm, w_ref[:, MIX_Q:MIX_Q + Q_LORA]), gql_ref[...]).astype(BF16)
    qs = _dot(qn, wq_ref[...])
    lane = lax.broadcasted_iota(jnp.int32, (1, LANES), 1)
    for h in range(MLA_HEADS):
        sl = slice(h * LANES, (h + 1) * LANES)
        xq = qs[:, sl]
        r = lax.rsqrt(jnp.sum(xq * xq, axis=-1, keepdims=True) * (1.0 / QK_HEAD_DIM) + EPS)
        y = _rope(xq * gqs_ref[:, sl], cos, sn, sp) * (r * q_scale)
        q_ref[0, h] = (y + shift_ref[...]).astype(BF16)

    kvn = _rms(_dot(hm, w_ref[:, MIX_KV:MIX_KV + KV_LORA]), gkvl_ref[...]).astype(BF16)
    kk = _dot(kvn, wk_ref[...])
    vt = lax.dot_general(wvt_ref[...], kvn, (((1,), (1,)), ((), ())), preferred_element_type=F32)
    kpe = _dot(hm, w_ref[:, MIX_KPE:MIX_KPE + LANES])
    ss_pe = jnp.sum(kpe * kpe, axis=-1, keepdims=True)
    pe = _rope(kpe * gkpe_ref[...], cos, sn, sp)
    k_one = jnp.where(lane == SHIFT_LANE, 1.0, 0.0)
    v_one = jnp.where(lax.broadcasted_iota(jnp.int32, (LANES, 1), 0) == V_HEAD_DIM, 1.0, 0.0)
    for h in range(MLA_HEADS):
        sl = slice(h * LANES, (h + 1) * LANES)
        xk = kk[:, sl]
        r = lax.rsqrt((jnp.sum(xk * xk, axis=-1, keepdims=True) + ss_pe) * (1.0 / QK_HEAD_DIM) + EPS)
        k_ref[0, h] = ((xk * gks_ref[:, sl] + pe) * r + k_one).astype(BF16)
        v_ref[0, h] = (vt[sl, :] + v_one).astype(BF16)

    ge = _gelu_tanh(_dot(hm, w_ref[:, MIX_SG:MIX_SG + 2 * SG_WIDTH]))
    u = ge[:, :SG_WIDTH]
    low = lane < SG_GROUP_DIM
    vparts = []
    for c in range(SG_WIDTH // LANES):
        vv = ge[:, SG_WIDTH + c * LANES:SG_WIDTH + (c + 1) * LANES]
        v2 = vv * vv
        s_lo = jnp.sum(jnp.where(low, v2, 0.0), axis=-1, keepdims=True)
        s_hi = jnp.sum(jnp.where(low, 0.0, v2), axis=-1, keepdims=True)
        r = jnp.where(low, lax.rsqrt(s_lo * (1.0 / SG_GROUP_DIM) + EPS), lax.rsqrt(s_hi * (1.0 / SG_GROUP_DIM) + EPS))
        vparts.append(vv * r * gsgu_ref[:, c * LANES:(c + 1) * LANES])
    vn = jnp.concatenate(vparts, axis=-1)
    group = lax.broadcasted_iota(jnp.int32, (1, SG_WIDTH), 1) // SG_GROUP_DIM
    sg_rows = []
    for c in range(tm // CHUNK):
        vc = vn[c * CHUNK:(c + 1) * CHUNK]
        stacked = jnp.concatenate([jnp.where(group == g, vc, 0.0) for g in range(SG_GROUPS)], axis=0)
        vs = _dot(wsp_ref[...], stacked.astype(BF16)) + bsp_ref[...]
        sg_rows.append(u[c * CHUNK:(c + 1) * CHUNK] * vs)
    sg = jnp.concatenate(sg_rows, axis=0)
    sgcv_ref[0, :, 0:SG_WIDTH] = _rms(sg, gosg_ref[...]).astype(BF16)

    cvin = _dot(h_scr[...], w_ref[:, MIX_CV:MIX_CV + 3 * CONV_WIDTH])
    z = cvin[:, CONV_WIDTH:2 * CONV_WIDTH] * cvin[:, 2 * CONV_WIDTH:]
    z_scr[...] = z
    z_scr[0:halo] = jnp.where(i > 0, z[0:halo], 0.0)
    z_scr[halo + tm:] = jnp.where(i < last, z[halo + tm:], 0.0)
    y = (wcv_ref[0:1, :] * z_scr[halo - 1:halo - 1 + tm] + wcv_ref[1:2, :] * z_scr[halo:halo + tm]
         + wcv_ref[2:3, :] * z_scr[halo + 1:halo + 1 + tm])
    cv = cvin[halo:halo + tm, 0:CONV_WIDTH] * y
    sgcv_ref[0, :, SG_WIDTH:] = _rms(cv, gocv_ref[...]).astype(BF16)


def _mix(x, mod, p, tables):
    bsz, t, d = x.shape
    tm = min(TOKEN_TILE, t)
    halo = BF16_ROWS
    nblk = t // halo
    per = tm // halo
    cos, sn, sp = tables
    const_args = [p["g_mix"], p["w_mix"], p["g_q_lat"], p["w_q"], p["g_kv_lat"], p["w_k"], p["w_vt"],
                  p["g_q_slot"], p["g_k_slot"], p["g_k_pe"], p["q_shift"]]
    tail_args = [p["g_sgu"], p["w_sp"], p["b_sp"], p["w_conv"], p["g_out_sg"], p["g_out_cv"]]
    tab_spec = pl.BlockSpec((tm, LANES), lambda b, i: (i, 0))
    in_specs = ([pl.BlockSpec((1, tm, d), lambda b, i: (b, i, 0)),
                 pl.BlockSpec((1, halo, d), lambda b, i: (b, jnp.maximum(i * per - 1, 0), 0)),
                 pl.BlockSpec((1, halo, d), lambda b, i: (b, jnp.minimum((i + 1) * per, nblk - 1), 0)),
                 pl.BlockSpec((1, ADA_CHUNKS, d), lambda b, i: (b, 0, 0))]
                + [_const_spec(a.shape) for a in const_args]
                + [tab_spec, tab_spec, tab_spec]
                + [_const_spec(a.shape) for a in tail_args])
    head_spec = pl.BlockSpec((1, MLA_HEADS, tm, LANES), lambda b, i: (b, 0, i, 0))
    return pl.pallas_call(
        functools.partial(_mix_kernel, tm=tm, q_scale=LOG2E * QK_HEAD_DIM ** -0.5),
        grid=(bsz, t // tm),
        in_specs=in_specs,
        out_specs=[head_spec, head_spec,
                   pl.BlockSpec((1, MLA_HEADS, LANES, tm), lambda b, i: (b, 0, 0, i)),
                   pl.BlockSpec((1, tm, SG_WIDTH + CONV_WIDTH), lambda b, i: (b, i, 0))],
        out_shape=[jax.ShapeDtypeStruct((bsz, MLA_HEADS, t, LANES), BF16),
                   jax.ShapeDtypeStruct((bsz, MLA_HEADS, t, LANES), BF16),
                   jax.ShapeDtypeStruct((bsz, MLA_HEADS, LANES, t), BF16),
                   jax.ShapeDtypeStruct((bsz, t, SG_WIDTH + CONV_WIDTH), BF16)],
        scratch_shapes=[pltpu.VMEM((tm + 2 * halo, d), BF16), pltpu.VMEM((tm + 2 * halo, CONV_WIDTH), F32)],
        compiler_params=pltpu.CompilerParams(vmem_limit_bytes=VMEM_LIMIT_BYTES),
        name="mixer_front",
    )(x, x, x, mod, *const_args, cos, sn, sp, *tail_args)


NEG_BIG = -1e30
VT_ROWS = 80


def _attn_kernel(*refs, tk, has_extra, online):
    if has_extra:
        q_ref, k_ref, vt_ref, k2_ref, vt2_ref, o_ref = refs
    else:
        q_ref, k_ref, vt_ref, o_ref = refs
    tq = q_ref.shape[2]
    qs = [q_ref[0, h] for h in range(2)]

    def step(qh, kh, vth, state):
        st = lax.dot_general(kh, qh, (((1,), (1,)), ((), ())), preferred_element_type=F32)
        if online:
            m, acc = state
            m_new = jnp.maximum(m, jnp.max(st, axis=0, keepdims=True))
            acc = jnp.exp2(m - m_new) * acc + _dot(vth, jnp.exp2(st - m_new).astype(BF16))
            return m_new, acc
        (acc,) = state
        return (acc + _dot(vth, jnp.exp2(st).astype(BF16)),)

    acc0 = jnp.zeros((VT_ROWS, tq), F32)
    init = (jnp.full((1, tq), NEG_BIG, F32), acc0) if online else (acc0,)
    states = (init, init)
    for j in range(k_ref.shape[2] // tk):
        ks = slice(j * tk, (j + 1) * tk)
        states = tuple(step(qs[h], k_ref[0, h, ks, :], vt_ref[0, h, 0:VT_ROWS, ks], states[h]) for h in range(2))
    if has_extra:
        states = tuple(step(qs[h], k2_ref[0, h], vt2_ref[0, h, 0:VT_ROWS, :], states[h]) for h in range(2))
    outs = []
    for st in states:
        acc = st[-1]
        outs.append(acc[0:V_HEAD_DIM] / acc[V_HEAD_DIM:V_HEAD_DIM + 1])
    o_ref[0] = jnp.concatenate(outs, axis=0).T


def _attention(q, k, v, k2=None, v2=None, *, online):
    bsz, heads, t, _ = q.shape
    t1 = k.shape[2]
    tq = min(Q_TILE, t)
    tk = min(KV_TILE, t1)
    has_extra = k2 is not None
    q_spec = pl.BlockSpec((1, 2, tq, LANES), lambda b, p, i: (b, p, i, 0))
    k_spec = lambda n: pl.BlockSpec((1, 2, n, LANES), lambda b, p, i: (b, p, 0, 0))
    vt_spec = lambda n: pl.BlockSpec((1, 2, LANES, n), lambda b, p, i: (b, p, 0, 0))
    in_specs = [q_spec, k_spec(t1), vt_spec(t1)]
    args = [q, k, v]
    if has_extra:
        in_specs += [k_spec(k2.shape[2]), vt_spec(k2.shape[2])]
        args += [k2, v2]
    return pl.pallas_call(
        functools.partial(_attn_kernel, tk=tk, has_extra=has_extra, online=online),
        grid=(bsz, heads // 2, t // tq),
        in_specs=in_specs,
        out_specs=pl.BlockSpec((1, tq, LANES), lambda b, p, i: (b, i, p)),
        out_shape=jax.ShapeDtypeStruct((bsz, t, ATT_WIDTH), F32),
        compiler_params=pltpu.CompilerParams(vmem_limit_bytes=VMEM_LIMIT_BYTES),
        name=("attention_online" if online else "attention_shifted") + ("_extra" if has_extra else ""),
    )(*args)


def _slot_pad(w, width):
    lead = w.shape[:-1]
    w = w.reshape(*lead, MLA_HEADS, width)
    w = jnp.pad(w, [(0, 0)] * len(lead) + [(0, 0), (0, LANES - width)])
    return w.reshape(*lead, MLA_HEADS * LANES)


def _prep_ffn(w_in, w_out):
    d, two_ff = w_in.shape
    d_ff = two_ff // 2
    n = d_ff // FF_CHUNK
    wa = w_in[:, :d_ff].reshape(d, n, FF_CHUNK)
    wb = w_in[:, d_ff:].reshape(d, n, FF_CHUNK)
    wab = jnp.concatenate([wa, wb], axis=-1).transpose(1, 0, 2).astype(BF16)
    return wab, w_out.reshape(n, FF_CHUNK, d).astype(BF16)


def _max_shift(g_q, g_k):
    return (LOG2E * QK_HEAD_DIM ** 0.5) * jnp.max(jnp.abs(g_q)) * jnp.max(jnp.abs(g_k))


def _prep_layer(l, w):
    row = lambda a: a.reshape(1, -1).astype(F32)
    w_mix_in = w["w_mix_in"][l]
    d = w_mix_in.shape[0]
    kpe_cols = jnp.zeros((d, LANES), F32).at[:, NOPE_DIM:QK_HEAD_DIM].set(w_mix_in[:, OFF_KPE:OFF_SG])
    w_mix = jnp.concatenate([w_mix_in[:, :OFF_KPE], w_mix_in[:, OFF_SG:], kpe_cols], axis=1).astype(BF16)
    w_kv = w["w_kv_up"][l].reshape(KV_LORA, MLA_HEADS, NOPE_DIM + V_HEAD_DIM)
    w_k = _slot_pad(w_kv[:, :, :NOPE_DIM].reshape(KV_LORA, -1), NOPE_DIM)
    w_v = _slot_pad(w_kv[:, :, NOPE_DIM:].reshape(KV_LORA, ATT_WIDTH), V_HEAD_DIM)
    g_k = w["g_k_head"][l]
    g_out = w["g_out"][l]
    w_sp = w["w_spatial"][l]
    ffn1 = _prep_ffn(w["w_ffn1_in"][l], w["w_ffn1_out"][l])
    ffn2 = _prep_ffn(w["w_ffn2_in"][l], w["w_ffn2_out"][l])
    return {
        "g_ffn1": row(w["g_ffn1"][l]), "ffn1_ab": ffn1[0], "ffn1_out": ffn1[1],
        "g_ffn2": row(w["g_ffn2"][l]), "ffn2_ab": ffn2[0], "ffn2_out": ffn2[1],
        "g_mix": row(w["g_mix"][l]), "w_mix": w_mix,
        "g_q_lat": row(w["g_q_lat"][l]), "w_q": _slot_pad(w["w_q_up"][l], QK_HEAD_DIM).astype(BF16),
        "g_kv_lat": row(w["g_kv_lat"][l]), "w_k": w_k.astype(BF16), "w_vt": w_v.T.astype(BF16),
        "g_q_slot": row(_slot_pad(jnp.tile(w["g_q_head"][l], MLA_HEADS), QK_HEAD_DIM)),
        "g_k_slot": row(_slot_pad(jnp.tile(g_k[:NOPE_DIM], MLA_HEADS), NOPE_DIM)),
        "g_k_pe": row(jnp.zeros((LANES,), F32).at[NOPE_DIM:QK_HEAD_DIM].set(g_k[NOPE_DIM:])),
        "q_shift": jnp.zeros((1, LANES), F32).at[0, SHIFT_LANE].set(-_max_shift(w["g_q_head"][l], g_k)),
        "g_sgu": row(w["g_sgu"][l]),
        "w_sp": jnp.concatenate([w_sp[g] for g in range(SG_GROUPS)], axis=1).astype(BF16),
        "b_sp": jnp.repeat(w["b_spatial"][l].T, SG_GROUP_DIM, axis=1).astype(F32),
        "w_conv": w["w_conv"][l].astype(F32),
        "g_out_att": row(g_out[:ATT_WIDTH]),
        "g_out_sg": row(g_out[ATT_WIDTH:ATT_WIDTH + SG_WIDTH]),
        "g_out_cv": row(g_out[ATT_WIDTH + SG_WIDTH:]),
        "w_mix_out": w["w_mix_out"][l].astype(BF16),
    }


def _rope_tables(n):
    rows = n // GRID_W
    row = jnp.repeat(jnp.arange(rows), GRID_W).astype(F32)
    col = jnp.tile(jnp.arange(GRID_W), rows).astype(F32)
    inv = 1.0 / (ROPE_BASE ** (jnp.arange(0, AXIS_DIM, 2, dtype=F32) / AXIS_DIM))
    ang_r = row[:, None] * inv
    ang_c = col[:, None] * inv
    ang = jnp.concatenate([ang_r, ang_r, ang_c, ang_c], axis=-1)
    cos, sin = jnp.cos(ang), jnp.sin(ang)
    first_half = (jnp.arange(ROPE_DIM) % AXIS_DIM) < AXIS_DIM // 2
    pad = lambda a, fill: jnp.pad(a, ((0, 0), (NOPE_DIM, LANES - QK_HEAD_DIM)), constant_values=fill)
    return (pad(cos, 1.0), pad(jnp.where(first_half, -sin, 0.0), 0.0), pad(jnp.where(first_half, 0.0, sin), 0.0))


def _identity_tables(n):
    return (jnp.ones((n, LANES), F32), jnp.zeros((n, LANES), F32), jnp.zeros((n, LANES), F32))


def kernel(x, c, ctx, c_ctx, w_ada, b_ada, g_ffn1, w_ffn1_in, w_ffn1_out, g_mix, w_mix_in, g_q_lat, w_q_up, g_kv_lat, w_kv_up, g_q_head, g_k_head, g_sgu, w_spatial, b_spatial, w_conv, g_out, w_mix_out, g_ffn2, w_ffn2_in, w_ffn2_out):
    weights = dict(g_ffn1=g_ffn1, w_ffn1_in=w_ffn1_in, w_ffn1_out=w_ffn1_out, g_mix=g_mix, w_mix_in=w_mix_in,
                   g_q_lat=g_q_lat, w_q_up=w_q_up, g_kv_lat=g_kv_lat, w_kv_up=w_kv_up, g_q_head=g_q_head,
                   g_k_head=g_k_head, g_sgu=g_sgu, w_spatial=w_spatial, b_spatial=b_spatial, w_conv=w_conv,
                   g_out=g_out, w_mix_out=w_mix_out, g_ffn2=g_ffn2, w_ffn2_in=w_ffn2_in, w_ffn2_out=w_ffn2_out)
    bsz, t, d = x.shape
    depth = w_ada.shape[0]
    rows = 8 * ((bsz + 1 + 7) // 8)
    cvec = jnp.zeros((rows, d), F32).at[:bsz].set(c).at[bsz].set(c_ctx)
    mod = _ada(cvec, w_ada, b_ada)
    lat_tables = _rope_tables(t)
    ctx_tables = _identity_tables(ctx.shape[1])

    h, hc = x, ctx
    for l in range(depth):
        last = l == depth - 1
        p = _prep_layer(l, weights)
        mod_l = mod[l, :bsz].reshape(bsz, ADA_CHUNKS, d)
        mod_c = jnp.broadcast_to(mod[l, bsz].reshape(1, ADA_CHUNKS, d), (bsz, ADA_CHUNKS, d))

        h = _ffn(h, mod_l, p["g_ffn1"], p["ffn1_ab"], p["ffn1_out"], mod_base=0)
        hc = _ffn(hc, mod_c, p["g_ffn1"], p["ffn1_ab"], p["ffn1_out"], mod_base=0)

        q, k, v, sgcv = _mix(h, mod_l, p, lat_tables)
        qc, kc, vc, sgcv_c = _mix(hc, mod_c, p, ctx_tables)
        attn = lax.cond(_max_shift(g_q_head[l], g_k_head[l]) <= MAX_SAFE_SHIFT,
                        functools.partial(_attention, online=False),
                        functools.partial(_attention, online=True), q, k, v, kc, vc)
        h = _ffn(h, mod_l, p["g_ffn2"], p["ffn2_ab"], p["ffn2_out"], mod_base=6,
                 merge_args=(attn, sgcv, p["g_out_att"], p["w_mix_out"]))
        if not last:
            attn_c = _attention(qc, kc, vc, online=True)
            hc = _ffn(hc, mod_c, p["g_ffn2"], p["ffn2_ab"], p["ffn2_out"], mod_base=6,
                      merge_args=(attn_c, sgcv_c, p["g_out_att"], p["w_mix_out"]))
    return h
```

```python
import functools

import jax
import jax.numpy as jnp
from jax import lax
from jax.experimental import pallas as pl
from jax.experimental.pallas import tpu as pltpu

EPS = 1e-6
ADA_CHUNKS = 9
FFN_RESIDUAL_WEIGHT = 0.5
GRID_W = 64
ROPE_BASE = 10000.0
MLA_HEADS = 8
NOPE_DIM = 64
ROPE_DIM = 32
AXIS_DIM = ROPE_DIM // 2
QK_HEAD_DIM = NOPE_DIM + ROPE_DIM
V_HEAD_DIM = 64
Q_LORA = 384
KV_LORA = 256
ATT_WIDTH = MLA_HEADS * V_HEAD_DIM
SG_GROUPS = 4
SG_GROUP_DIM = 64
SG_WIDTH = SG_GROUPS * SG_GROUP_DIM
CHUNK = 128
CONV_WIDTH = 256
OFF_KV = Q_LORA
OFF_KPE = OFF_KV + KV_LORA
OFF_SG = OFF_KPE + ROPE_DIM
OFF_CONV = OFF_SG + 2 * SG_WIDTH
LOG2E = 1.4426950408889634
SHIFT_LANE = QK_HEAD_DIM
MAX_SAFE_SHIFT = 60.0

LANES = 128
BF16_ROWS = 16
VMEM_LIMIT_BYTES = 56 * 1024 * 1024

MIX_Q = 0
MIX_KPE = MIX_Q + Q_LORA
MIX_KV = MIX_KPE + LANES
MIX_SG = MIX_KV + KV_LORA
MIX_CV = MIX_SG + 2 * SG_WIDTH
MIX_COLS = MIX_CV + 3 * CONV_WIDTH

FF_CHUNK = 256
TOKEN_TILE = 512
Q_TILE = 512
KV_TILE = 256

BF16 = jnp.bfloat16
F32 = jnp.float32


def _dot(a, b):
    return jnp.dot(a, b, preferred_element_type=F32)


def _rms(x, g):
    return x * lax.rsqrt(jnp.mean(x * x, axis=-1, keepdims=True) + EPS) * g


def _silu(x):
    return x * jax.nn.sigmoid(x)


def _gelu_tanh(x):
    return 0.5 * x * (1.0 + jnp.tanh(0.7978845608028654 * (x + 0.044715 * (x * x * x))))


def _const_spec(shape):
    zeros = (0,) * len(shape)
    return pl.BlockSpec(shape, lambda *_: zeros, pipeline_mode=pl.Buffered(1))


def _ada_kernel(c_ref, w_ref, b_ref, o_ref):
    s = _silu(c_ref[...])
    w = w_ref[0]
    s_hi = s.astype(BF16)
    s_lo = (s - s_hi.astype(F32)).astype(BF16)
    w_hi = w.astype(BF16)
    w_lo = (w - w_hi.astype(F32)).astype(BF16)
    o_ref[0] = _dot(s_hi, w_hi) + _dot(s_lo, w_hi) + _dot(s_hi, w_lo) + b_ref[0]


def _ada(cvec, w_ada, b_ada):
    depth, d, n = w_ada.shape
    rows = cvec.shape[0]
    tn = d
    return pl.pallas_call(
        _ada_kernel,
        grid=(depth, n // tn),
        in_specs=[
            pl.BlockSpec((rows, d), lambda l, j: (0, 0)),
            pl.BlockSpec((1, d, tn), lambda l, j: (l, 0, j)),
            pl.BlockSpec((1, 1, tn), lambda l, j: (l, 0, j)),
        ],
        out_specs=pl.BlockSpec((1, rows, tn), lambda l, j: (l, 0, j)),
        out_shape=jax.ShapeDtypeStruct((depth, rows, n), F32),
        compiler_params=pltpu.CompilerParams(vmem_limit_bytes=VMEM_LIMIT_BYTES),
        name="ada_modulation",
    )(cvec, w_ada, b_ada.reshape(depth, 1, n))


def _ffn_kernel(*refs, merge, mod_base):
    if merge:
        (x_ref, mod_ref, attn_ref, sgcv_ref, gatt_ref, wmo_ref, g_ref, win_ref, wout_ref,
         o_ref, h_scr, acc_scr) = refs
    else:
        x_ref, mod_ref, g_ref, win_ref, wout_ref, o_ref, h_scr, acc_scr = refs
    d_ff = wout_ref.shape[0]
    x = x_ref[0]
    if merge:
        att = _rms(attn_ref[0], gatt_ref[...]).astype(BF16)
        y = _dot(att, wmo_ref[0:ATT_WIDTH, :]) + _dot(sgcv_ref[0], wmo_ref[ATT_WIDTH:, :])
        x = x + mod_ref[0, 5:6, :] * y
    o_ref[0] = x
    shift = mod_ref[0, mod_base:mod_base + 1, :]
    scale = mod_ref[0, mod_base + 1:mod_base + 2, :]
    h_scr[...] = (_rms(x, g_ref[...]) * (1.0 + scale) + shift).astype(BF16)

    def chunk(c):
        a = _dot(h_scr[...], win_ref[:, c:c + FF_CHUNK])
        b = _dot(h_scr[...], win_ref[:, d_ff + c:d_ff + c + FF_CHUNK])
        return _dot((_silu(a) * b).astype(BF16), wout_ref[c:c + FF_CHUNK, :])

    acc_scr[...] = chunk(0)
    for c in range(FF_CHUNK, d_ff, FF_CHUNK):
        acc_scr[...] += chunk(c)
    gate = mod_ref[0, mod_base + 2:mod_base + 3, :]
    o_ref[0] = o_ref[0] + (FFN_RESIDUAL_WEIGHT * gate) * acc_scr[...]


def _ffn(x, mod, g, w_in, wout, *, mod_base, merge_args=None):
    bsz, t, d = x.shape
    tm = min(TOKEN_TILE, t)
    assert wout.shape[0] % FF_CHUNK == 0
    tile = lambda w: pl.BlockSpec((1, tm, w), lambda b, i: (b, i, 0))
    in_specs = [tile(d), pl.BlockSpec((1, ADA_CHUNKS, d), lambda b, i: (b, 0, 0))]
    args = [x, mod]
    if merge_args is not None:
        attn, sgcv, g_att, wmo = merge_args
        in_specs += [tile(ATT_WIDTH), tile(SG_WIDTH + CONV_WIDTH), _const_spec(g_att.shape), _const_spec(wmo.shape)]
        args += [attn, sgcv, g_att, wmo]
    in_specs += [_const_spec(g.shape), _const_spec(w_in.shape), _const_spec(wout.shape)]
    args += [g, w_in, wout]
    return pl.pallas_call(
        functools.partial(_ffn_kernel, merge=merge_args is not None, mod_base=mod_base),
        grid=(bsz, t // tm),
        in_specs=in_specs,
        out_specs=tile(d),
        out_shape=jax.ShapeDtypeStruct(x.shape, F32),
        scratch_shapes=[pltpu.VMEM((tm, d), BF16), pltpu.VMEM((tm, d), F32)],
        compiler_params=pltpu.CompilerParams(vmem_limit_bytes=VMEM_LIMIT_BYTES),
        name="ffn_merge" if merge_args is not None else "ffn",
    )(*args)


def _rope(xg, cos, sin_next, sin_prev):
    return xg * cos + pltpu.roll(xg, LANES - 8, 1) * sin_next + pltpu.roll(xg, 8, 1) * sin_prev


def _mix_kernel(x_ref, xp_ref, xn_ref, mod_ref, gmix_ref, w_ref, gql_ref, wqt_ref, gkvl_ref, wk_ref, wvt_ref,
                gqc_ref, gks_ref, gkpe_ref, shift_ref, cos_ref, sn_ref, sp_ref, cost_ref, sint_ref,
                gsgu_ref, wsp_ref, bsp_ref,
                wcv_ref, gosg_ref, gocv_ref, q_ref, k_ref, v_ref, sgcv_ref, h_scr, z_scr, *, tm, q_scale):
    i = pl.program_id(1)
    last = pl.num_programs(1) - 1
    halo = BF16_ROWS
    shift = mod_ref[0, 3:4, :]
    scale1 = 1.0 + mod_ref[0, 4:5, :]
    gmix = gmix_ref[...]

    def norm_mod(x):
        return (_rms(x, gmix) * scale1 + shift).astype(BF16)

    h_scr[0:halo] = norm_mod(xp_ref[0])
    h_scr[halo:halo + tm] = norm_mod(x_ref[0])
    h_scr[halo + tm:] = norm_mod(xn_ref[0])
    hm = h_scr[halo:halo + tm]

    cos = cos_ref[...]
    sn = sn_ref[...]
    sp = sp_ref[...]

    q_kpe = _dot(hm, w_ref[:, MIX_Q:MIX_KV])
    qn = _rms(q_kpe[:, :Q_LORA], gql_ref[...]).astype(BF16)
    qst = lax.dot_general(wqt_ref[...], qn, (((1,), (1,)), ((), ())), preferred_element_type=F32)
    cost = cost_ref[...]
    sint = sint_ref[...]
    gq = gqc_ref[...]
    lane = lax.broadcasted_iota(jnp.int32, (1, LANES), 1)
    half = AXIS_DIM // 2
    for h in range(MLA_HEADS):
        xq = qst[h * LANES:(h + 1) * LANES]
        r = lax.rsqrt(jnp.sum(xq * xq, axis=0, keepdims=True) * (1.0 / QK_HEAD_DIM) + EPS) * q_scale
        xg = xq * gq
        pe = xg[NOPE_DIM:QK_HEAD_DIM]
        swapped = jnp.concatenate([pe[half:2 * half], pe[0:half], pe[3 * half:4 * half], pe[2 * half:3 * half]], axis=0)
        y = jnp.concatenate([xg[0:NOPE_DIM], pe * cost + swapped * sint, xg[QK_HEAD_DIM:]], axis=0)
        q_ref[0, h] = (y * r + shift_ref[...]).astype(BF16)

    kvn = _rms(_dot(hm, w_ref[:, MIX_KV:MIX_KV + KV_LORA]), gkvl_ref[...]).astype(BF16)
    kk = _dot(kvn, wk_ref[...])
    vt = lax.dot_general(wvt_ref[...], kvn, (((1,), (1,)), ((), ())), preferred_element_type=F32)
    kpe = q_kpe[:, Q_LORA:]
    ss_pe = jnp.sum(kpe * kpe, axis=-1, keepdims=True)
    pe = _rope(kpe * gkpe_ref[...], cos, sn, sp)
    k_one = jnp.where(lane == SHIFT_LANE, 1.0, 0.0)
    v_one = jnp.where(lax.broadcasted_iota(jnp.int32, (LANES, 1), 0) == V_HEAD_DIM, 1.0, 0.0)
    for h in range(MLA_HEADS):
        sl = slice(h * LANES, (h + 1) * LANES)
        xk = kk[:, sl]
        r = lax.rsqrt((jnp.sum(xk * xk, axis=-1, keepdims=True) + ss_pe) * (1.0 / QK_HEAD_DIM) + EPS)
        k_ref[0, h] = ((xk * gks_ref[:, sl] + pe) * r + k_one).astype(BF16)
        v_ref[0, h] = (vt[sl, :] + v_one).astype(BF16)

    ge = _gelu_tanh(_dot(hm, w_ref[:, MIX_SG:MIX_SG + 2 * SG_WIDTH]))
    u = ge[:, :SG_WIDTH]
    low = lane < SG_GROUP_DIM
    vparts = []
    for c in range(SG_WIDTH // LANES):
        vv = ge[:, SG_WIDTH + c * LANES:SG_WIDTH + (c + 1) * LANES]
        v2 = vv * vv
        s_lo = jnp.sum(jnp.where(low, v2, 0.0), axis=-1, keepdims=True)
        s_hi = jnp.sum(jnp.where(low, 0.0, v2), axis=-1, keepdims=True)
        r = jnp.where(low, lax.rsqrt(s_lo * (1.0 / SG_GROUP_DIM) + EPS), lax.rsqrt(s_hi * (1.0 / SG_GROUP_DIM) + EPS))
        vparts.append(vv * r * gsgu_ref[:, c * LANES:(c + 1) * LANES])
    vn = jnp.concatenate(vparts, axis=-1)
    group = lax.broadcasted_iota(jnp.int32, (1, SG_WIDTH), 1) // SG_GROUP_DIM
    sg_rows = []
    for c in range(tm // CHUNK):
        vc = vn[c * CHUNK:(c + 1) * CHUNK]
        stacked = jnp.concatenate([jnp.where(group == g, vc, 0.0) for g in range(SG_GROUPS)], axis=0)
        vs = _dot(wsp_ref[...], stacked.astype(BF16)) + bsp_ref[...]
        sg_rows.append(u[c * CHUNK:(c + 1) * CHUNK] * vs)
    sg = jnp.concatenate(sg_rows, axis=0)
    sgcv_ref[0, :, 0:SG_WIDTH] = _rms(sg, gosg_ref[...]).astype(BF16)

    cvin = _dot(h_scr[...], w_ref[:, MIX_CV:MIX_CV + 3 * CONV_WIDTH])
    z = cvin[:, CONV_WIDTH:2 * CONV_WIDTH] * cvin[:, 2 * CONV_WIDTH:]
    z_scr[...] = z
    z_scr[0:halo] = jnp.where(i > 0, z[0:halo], 0.0)
    z_scr[halo + tm:] = jnp.where(i < last, z[halo + tm:], 0.0)
    y = (wcv_ref[0:1, :] * z_scr[halo - 1:halo - 1 + tm] + wcv_ref[1:2, :] * z_scr[halo:halo + tm]
         + wcv_ref[2:3, :] * z_scr[halo + 1:halo + 1 + tm])
    cv = cvin[halo:halo + tm, 0:CONV_WIDTH] * y
    sgcv_ref[0, :, SG_WIDTH:] = _rms(cv, gocv_ref[...]).astype(BF16)


def _mix(x, mod, p, tables):
    bsz, t, d = x.shape
    tm = min(TOKEN_TILE, t)
    halo = BF16_ROWS
    nblk = t // halo
    per = tm // halo
    cos, sn, sp, cost, sint = tables
    const_args = [p["g_mix"], p["w_mix"], p["g_q_lat"], p["w_qt"], p["g_kv_lat"], p["w_k"], p["w_vt"],
                  p["g_q_col"], p["g_k_slot"], p["g_k_pe"], p["q_shift"]]
    tail_args = [p["g_sgu"], p["w_sp"], p["b_sp"], p["w_conv"], p["g_out_sg"], p["g_out_cv"]]
    tab_spec = pl.BlockSpec((tm, LANES), lambda b, i: (i, 0))
    tabt_spec = pl.BlockSpec((ROPE_DIM, tm), lambda b, i: (0, i))
    in_specs = ([pl.BlockSpec((1, tm, d), lambda b, i: (b, i, 0)),
                 pl.BlockSpec((1, halo, d), lambda b, i: (b, jnp.maximum(i * per - 1, 0), 0)),
                 pl.BlockSpec((1, halo, d), lambda b, i: (b, jnp.minimum((i + 1) * per, nblk - 1), 0)),
                 pl.BlockSpec((1, ADA_CHUNKS, d), lambda b, i: (b, 0, 0))]
                + [_const_spec(a.shape) for a in const_args]
                + [tab_spec, tab_spec, tab_spec, tabt_spec, tabt_spec]
                + [_const_spec(a.shape) for a in tail_args])
    head_spec = pl.BlockSpec((1, MLA_HEADS, tm, LANES), lambda b, i: (b, 0, i, 0))
    headt_spec = pl.BlockSpec((1, MLA_HEADS, LANES, tm), lambda b, i: (b, 0, 0, i))
    return pl.pallas_call(
        functools.partial(_mix_kernel, tm=tm, q_scale=LOG2E * QK_HEAD_DIM ** -0.5),
        grid=(bsz, t // tm),
        in_specs=in_specs,
        out_specs=[headt_spec, head_spec, headt_spec,
                   pl.BlockSpec((1, tm, SG_WIDTH + CONV_WIDTH), lambda b, i: (b, i, 0))],
        out_shape=[jax.ShapeDtypeStruct((bsz, MLA_HEADS, LANES, t), BF16),
                   jax.ShapeDtypeStruct((bsz, MLA_HEADS, t, LANES), BF16),
                   jax.ShapeDtypeStruct((bsz, MLA_HEADS, LANES, t), BF16),
                   jax.ShapeDtypeStruct((bsz, t, SG_WIDTH + CONV_WIDTH), BF16)],
        scratch_shapes=[pltpu.VMEM((tm + 2 * halo, d), BF16), pltpu.VMEM((tm + 2 * halo, CONV_WIDTH), F32)],
        compiler_params=pltpu.CompilerParams(vmem_limit_bytes=VMEM_LIMIT_BYTES),
        name="mixer_front",
    )(x, x, x, mod, *const_args, cos, sn, sp, cost, sint, *tail_args)


NEG_BIG = -1e30
VT_ROWS = 80
SCORE_LEAD = 1


def _attn_kernel(*refs, tk, has_extra, online):
    if has_extra:
        q_ref, k_ref, vt_ref, k2_ref, vt2_ref, o_ref = refs
    else:
        q_ref, k_ref, vt_ref, o_ref = refs
    tq = q_ref.shape[3]
    qs = [q_ref[0, h] for h in range(2)]

    def consume(st, vth, state):
        if online:
            m, acc = state
            m_new = jnp.maximum(m, jnp.max(st, axis=0, keepdims=True))
            acc = jnp.exp2(m - m_new) * acc + _dot(vth, jnp.exp2(st - m_new).astype(BF16))
            return m_new, acc
        (acc,) = state
        return (acc + _dot(vth, jnp.exp2(st).astype(BF16)),)

    chunks = []
    for j in range(k_ref.shape[2] // tk):
        ks = slice(j * tk, (j + 1) * tk)
        chunks.append((lambda h, ks=ks: k_ref[0, h, ks, :], lambda h, ks=ks: vt_ref[0, h, 0:VT_ROWS, ks]))
    if has_extra:
        chunks.append((lambda h: k2_ref[0, h], lambda h: vt2_ref[0, h, 0:VT_ROWS, :]))

    acc0 = jnp.zeros((VT_ROWS, tq), F32)
    init = (jnp.full((1, tq), NEG_BIG, F32), acc0) if online else (acc0,)
    states = [init, init]
    n = len(chunks)
    score = lambda j: [_dot(chunks[j][0](h), qs[h]) for h in range(2)]
    pending = [score(j) for j in range(min(SCORE_LEAD, n))]
    for j in range(n):
        if j + SCORE_LEAD < n:
            pending.append(score(j + SCORE_LEAD))
        scores = pending.pop(0)
        states = [consume(scores[h], chunks[j][1](h), states[h]) for h in range(2)]
    outs = []
    for st in states:
        acc = st[-1]
        outs.append(acc[0:V_HEAD_DIM] / acc[V_HEAD_DIM:V_HEAD_DIM + 1])
    o_ref[0] = jnp.concatenate(outs, axis=0).T


def _attention(q, k, v, k2=None, v2=None, *, online):
    bsz, heads, _, t = q.shape
    t1 = k.shape[2]
    tq = min(Q_TILE, t)
    tk = min(KV_TILE, t1)
    has_extra = k2 is not None
    q_spec = pl.BlockSpec((1, 2, LANES, tq), lambda b, p, i: (b, p, 0, i))
    k_spec = lambda n: pl.BlockSpec((1, 2, n, LANES), lambda b, p, i: (b, p, 0, 0))
    vt_spec = lambda n: pl.BlockSpec((1, 2, LANES, n), lambda b, p, i: (b, p, 0, 0))
    in_specs = [q_spec, k_spec(t1), vt_spec(t1)]
    args = [q, k, v]
    if has_extra:
        in_specs += [k_spec(k2.shape[2]), vt_spec(k2.shape[2])]
        args += [k2, v2]
    return pl.pallas_call(
        functools.partial(_attn_kernel, tk=tk, has_extra=has_extra, online=online),
        grid=(bsz, heads // 2, t // tq),
        in_specs=in_specs,
        out_specs=pl.BlockSpec((1, tq, LANES), lambda b, p, i: (b, i, p)),
        out_shape=jax.ShapeDtypeStruct((bsz, t, ATT_WIDTH), F32),
        compiler_params=pltpu.CompilerParams(vmem_limit_bytes=VMEM_LIMIT_BYTES),
        name=("attention_online" if online else "attention_shifted") + ("_extra" if has_extra else ""),
    )(*args)


def _slot_pad(w, width):
    lead = w.shape[:-1]
    w = w.reshape(*lead, MLA_HEADS, width)
    w = jnp.pad(w, [(0, 0)] * len(lead) + [(0, 0), (0, LANES - width)])
    return w.reshape(*lead, MLA_HEADS * LANES)


def _prep_ffn(w_in, w_out):
    return w_in.astype(BF16), w_out.astype(BF16)


def _max_shift(g_q, g_k):
    return (LOG2E * QK_HEAD_DIM ** 0.5) * jnp.max(jnp.abs(g_q)) * jnp.max(jnp.abs(g_k))


def _prep_layer(l, w):
    row = lambda a: a.reshape(1, -1).astype(F32)
    w_mix_in = w["w_mix_in"][l]
    d = w_mix_in.shape[0]
    kpe_cols = jnp.zeros((d, LANES), F32).at[:, NOPE_DIM:QK_HEAD_DIM].set(w_mix_in[:, OFF_KPE:OFF_SG])
    w_mix = jnp.concatenate([w_mix_in[:, :OFF_KV], kpe_cols, w_mix_in[:, OFF_KV:OFF_KPE], w_mix_in[:, OFF_SG:]],
                            axis=1).astype(BF16)
    w_kv = w["w_kv_up"][l].reshape(KV_LORA, MLA_HEADS, NOPE_DIM + V_HEAD_DIM)
    w_k = _slot_pad(w_kv[:, :, :NOPE_DIM].reshape(KV_LORA, -1), NOPE_DIM)
    w_v = _slot_pad(w_kv[:, :, NOPE_DIM:].reshape(KV_LORA, ATT_WIDTH), V_HEAD_DIM)
    g_k = w["g_k_head"][l]
    g_out = w["g_out"][l]
    w_sp = w["w_spatial"][l]
    ffn1 = _prep_ffn(w["w_ffn1_in"][l], w["w_ffn1_out"][l])
    ffn2 = _prep_ffn(w["w_ffn2_in"][l], w["w_ffn2_out"][l])
    return {
        "g_ffn1": row(w["g_ffn1"][l]), "ffn1_ab": ffn1[0], "ffn1_out": ffn1[1],
        "g_ffn2": row(w["g_ffn2"][l]), "ffn2_ab": ffn2[0], "ffn2_out": ffn2[1],
        "g_mix": row(w["g_mix"][l]), "w_mix": w_mix,
        "g_q_lat": row(w["g_q_lat"][l]), "w_qt": _slot_pad(w["w_q_up"][l], QK_HEAD_DIM).T.astype(BF16),
        "g_kv_lat": row(w["g_kv_lat"][l]), "w_k": w_k.astype(BF16), "w_vt": w_v.T.astype(BF16),
        "g_q_col": jnp.pad(w["g_q_head"][l].astype(F32), (0, LANES - QK_HEAD_DIM)).reshape(LANES, 1),
        "g_k_slot": row(_slot_pad(jnp.tile(g_k[:NOPE_DIM], MLA_HEADS), NOPE_DIM)),
        "g_k_pe": row(jnp.zeros((LANES,), F32).at[NOPE_DIM:QK_HEAD_DIM].set(g_k[NOPE_DIM:])),
        "q_shift": jnp.zeros((LANES, 1), F32).at[SHIFT_LANE, 0].set(-_max_shift(w["g_q_head"][l], g_k)),
        "g_sgu": row(w["g_sgu"][l]),
        "w_sp": jnp.concatenate([w_sp[g] for g in range(SG_GROUPS)], axis=1).astype(BF16),
        "b_sp": jnp.repeat(w["b_spatial"][l].T, SG_GROUP_DIM, axis=1).astype(F32),
        "w_conv": w["w_conv"][l].astype(F32),
        "g_out_att": row(g_out[:ATT_WIDTH]),
        "g_out_sg": row(g_out[ATT_WIDTH:ATT_WIDTH + SG_WIDTH]),
        "g_out_cv": row(g_out[ATT_WIDTH + SG_WIDTH:]),
        "w_mix_out": w["w_mix_out"][l].astype(BF16),
    }


def _rope_tables(n):
    rows = n // GRID_W
    row = jnp.repeat(jnp.arange(rows), GRID_W).astype(F32)
    col = jnp.tile(jnp.arange(GRID_W), rows).astype(F32)
    inv = 1.0 / (ROPE_BASE ** (jnp.arange(0, AXIS_DIM, 2, dtype=F32) / AXIS_DIM))
    ang_r = row[:, None] * inv
    ang_c = col[:, None] * inv
    ang = jnp.concatenate([ang_r, ang_r, ang_c, ang_c], axis=-1)
    cos, sin = jnp.cos(ang), jnp.sin(ang)
    first_half = (jnp.arange(ROPE_DIM) % AXIS_DIM) < AXIS_DIM // 2
    pad = lambda a, fill: jnp.pad(a, ((0, 0), (NOPE_DIM, LANES - QK_HEAD_DIM)), constant_values=fill)
    return (pad(cos, 1.0), pad(jnp.where(first_half, -sin, 0.0), 0.0), pad(jnp.where(first_half, 0.0, sin), 0.0),
            cos.T, jnp.where(first_half, -sin, sin).T)


def _identity_tables(n):
    return (jnp.ones((n, LANES), F32), jnp.zeros((n, LANES), F32), jnp.zeros((n, LANES), F32),
            jnp.ones((ROPE_DIM, n), F32), jnp.zeros((ROPE_DIM, n), F32))


def kernel(x, c, ctx, c_ctx, w_ada, b_ada, g_ffn1, w_ffn1_in, w_ffn1_out, g_mix, w_mix_in, g_q_lat, w_q_up, g_kv_lat, w_kv_up, g_q_head, g_k_head, g_sgu, w_spatial, b_spatial, w_conv, g_out, w_mix_out, g_ffn2, w_ffn2_in, w_ffn2_out):
    weights = dict(g_ffn1=g_ffn1, w_ffn1_in=w_ffn1_in, w_ffn1_out=w_ffn1_out, g_mix=g_mix, w_mix_in=w_mix_in,
                   g_q_lat=g_q_lat, w_q_up=w_q_up, g_kv_lat=g_kv_lat, w_kv_up=w_kv_up, g_q_head=g_q_head,
                   g_k_head=g_k_head, g_sgu=g_sgu, w_spatial=w_spatial, b_spatial=b_spatial, w_conv=w_conv,
                   g_out=g_out, w_mix_out=w_mix_out, g_ffn2=g_ffn2, w_ffn2_in=w_ffn2_in, w_ffn2_out=w_ffn2_out)
    bsz, t, d = x.shape
    depth = w_ada.shape[0]
    rows = 8 * ((bsz + 1 + 7) // 8)
    cvec = jnp.zeros((rows, d), F32).at[:bsz].set(c).at[bsz].set(c_ctx)
    mod = _ada(cvec, w_ada, b_ada)
    lat_tables = _rope_tables(t)
    ctx_tables = _identity_tables(ctx.shape[1])

    h, hc = x, ctx
    for l in range(depth):
        last = l == depth - 1
        p = _prep_layer(l, weights)
        mod_l = mod[l, :bsz].reshape(bsz, ADA_CHUNKS, d)
        mod_c1 = mod[l, bsz].reshape(1, ADA_CHUNKS, d)
        mod_c = jnp.broadcast_to(mod_c1, (bsz, ADA_CHUNKS, d))
        flat = lambda a: a.reshape(1, -1, a.shape[-1])

        h = _ffn(h, mod_l, p["g_ffn1"], p["ffn1_ab"], p["ffn1_out"], mod_base=0)
        hc = _ffn(flat(hc), mod_c1, p["g_ffn1"], p["ffn1_ab"], p["ffn1_out"], mod_base=0).reshape(ctx.shape)

        q, k, v, sgcv = _mix(h, mod_l, p, lat_tables)
        qc, kc, vc, sgcv_c = _mix(hc, mod_c, p, ctx_tables)
        attn = lax.cond(_max_shift(g_q_head[l], g_k_head[l]) <= MAX_SAFE_SHIFT,
                        functools.partial(_attention, online=False),
                        functools.partial(_attention, online=True), q, k, v, kc, vc)
        h = _ffn(h, mod_l, p["g_ffn2"], p["ffn2_ab"], p["ffn2_out"], mod_base=6,
                 merge_args=(attn, sgcv, p["g_out_att"], p["w_mix_out"]))
        if not last:
            attn_c = _attention(qc, kc, vc, online=True)
            hc = _ffn(flat(hc), mod_c1, p["g_ffn2"], p["ffn2_ab"], p["ffn2_out"], mod_base=6,
                      merge_args=(flat(attn_c), flat(sgcv_c), p["g_out_att"], p["w_mix_out"])).reshape(ctx.shape)
    return h
```

```python
import functools

import jax
import jax.numpy as jnp
from jax import lax
from jax.experimental import pallas as pl
from jax.experimental.pallas import tpu as pltpu

EPS = 1e-6
ADA_CHUNKS = 9
FFN_RESIDUAL_WEIGHT = 0.5
GRID_W = 64
ROPE_BASE = 10000.0
MLA_HEADS = 8
NOPE_DIM = 64
ROPE_DIM = 32
AXIS_DIM = ROPE_DIM // 2
QK_HEAD_DIM = NOPE_DIM + ROPE_DIM
V_HEAD_DIM = 64
Q_LORA = 384
KV_LORA = 256
ATT_WIDTH = MLA_HEADS * V_HEAD_DIM
SG_GROUPS = 4
SG_GROUP_DIM = 64
SG_WIDTH = SG_GROUPS * SG_GROUP_DIM
CHUNK = 128
CONV_WIDTH = 256
OFF_KV = Q_LORA
OFF_KPE = OFF_KV + KV_LORA
OFF_SG = OFF_KPE + ROPE_DIM
OFF_CONV = OFF_SG + 2 * SG_WIDTH
LOG2E = 1.4426950408889634
SHIFT_LANE = QK_HEAD_DIM
MAX_SAFE_SHIFT = 60.0

LANES = 128
BF16_ROWS = 16
VMEM_LIMIT_BYTES = 56 * 1024 * 1024

MIX_Q = 0
MIX_KPE = MIX_Q + Q_LORA
MIX_KV = MIX_KPE + LANES
MIX_SG = MIX_KV + KV_LORA
MIX_CV = MIX_SG + 2 * SG_WIDTH
MIX_COLS = MIX_CV + 3 * CONV_WIDTH

FF_CHUNK = 256
TOKEN_TILE = 512
Q_TILE = 512
KV_TILE = 256

BF16 = jnp.bfloat16
F32 = jnp.float32


def _dot(a, b):
    return jnp.dot(a, b, preferred_element_type=F32)


def _rms(x, g):
    return x * lax.rsqrt(jnp.mean(x * x, axis=-1, keepdims=True) + EPS) * g


def _silu(x):
    return x * jax.nn.sigmoid(x)


def _gelu_tanh(x):
    return 0.5 * x * (1.0 + jnp.tanh(0.7978845608028654 * (x + 0.044715 * (x * x * x))))


def _const_spec(shape):
    zeros = (0,) * len(shape)
    return pl.BlockSpec(shape, lambda *_: zeros, pipeline_mode=pl.Buffered(1))


def _ada_kernel(c_ref, w_ref, b_ref, o_ref):
    s = _silu(c_ref[...])
    w = w_ref[0]
    s_hi = s.astype(BF16)
    s_lo = (s - s_hi.astype(F32)).astype(BF16)
    w_hi = w.astype(BF16)
    w_lo = (w - w_hi.astype(F32)).astype(BF16)
    o_ref[0] = _dot(s_hi, w_hi) + _dot(s_lo, w_hi) + _dot(s_hi, w_lo) + b_ref[0]


def _ada(cvec, w_ada, b_ada):
    depth, d, n = w_ada.shape
    rows = cvec.shape[0]
    tn = d
    return pl.pallas_call(
        _ada_kernel,
        grid=(depth, n // tn),
        in_specs=[
            pl.BlockSpec((rows, d), lambda l, j: (0, 0)),
            pl.BlockSpec((1, d, tn), lambda l, j: (l, 0, j)),
            pl.BlockSpec((1, 1, tn), lambda l, j: (l, 0, j)),
        ],
        out_specs=pl.BlockSpec((1, rows, tn), lambda l, j: (l, 0, j)),
        out_shape=jax.ShapeDtypeStruct((depth, rows, n), F32),
        compiler_params=pltpu.CompilerParams(vmem_limit_bytes=VMEM_LIMIT_BYTES),
        name="ada_modulation",
    )(cvec, w_ada, b_ada.reshape(depth, 1, n))


def _ffn_kernel(*refs, merge, mod_base):
    if merge:
        (x_ref, mod_ref, attn_ref, sgcv_ref, gatt_ref, wmo_ref, g_ref, win_ref, wout_ref,
         o_ref, h_scr, acc_scr) = refs
    else:
        x_ref, mod_ref, g_ref, win_ref, wout_ref, o_ref, h_scr, acc_scr = refs
    d_ff = wout_ref.shape[0]
    x = x_ref[0]
    if merge:
        att = _rms(attn_ref[0], gatt_ref[...]).astype(BF16)
        y = _dot(att, wmo_ref[0:ATT_WIDTH, :]) + _dot(sgcv_ref[0], wmo_ref[ATT_WIDTH:, :])
        x = x + mod_ref[0, 5:6, :] * y
    o_ref[0] = x
    shift = mod_ref[0, mod_base:mod_base + 1, :]
    scale = mod_ref[0, mod_base + 1:mod_base + 2, :]
    h_scr[...] = (_rms(x, g_ref[...]) * (1.0 + scale) + shift).astype(BF16)

    def chunk(c):
        a = _dot(h_scr[...], win_ref[:, c:c + FF_CHUNK])
        b = _dot(h_scr[...], win_ref[:, d_ff + c:d_ff + c + FF_CHUNK])
        return _dot((_silu(a) * b).astype(BF16), wout_ref[c:c + FF_CHUNK, :])

    acc_scr[...] = chunk(0)
    for c in range(FF_CHUNK, d_ff, FF_CHUNK):
        acc_scr[...] += chunk(c)
    gate = mod_ref[0, mod_base + 2:mod_base + 3, :]
    o_ref[0] = o_ref[0] + (FFN_RESIDUAL_WEIGHT * gate) * acc_scr[...]


def _ffn(x, mod, g, w_in, wout, *, mod_base, merge_args=None):
    bsz, t, d = x.shape
    tm = min(TOKEN_TILE, t)
    assert wout.shape[0] % FF_CHUNK == 0
    tile = lambda w: pl.BlockSpec((1, tm, w), lambda b, i: (b, i, 0))
    in_specs = [tile(d), pl.BlockSpec((1, ADA_CHUNKS, d), lambda b, i: (b, 0, 0))]
    args = [x, mod]
    if merge_args is not None:
        attn, sgcv, g_att, wmo = merge_args
        in_specs += [tile(ATT_WIDTH), tile(SG_WIDTH + CONV_WIDTH), _const_spec(g_att.shape), _const_spec(wmo.shape)]
        args += [attn, sgcv, g_att, wmo]
    in_specs += [_const_spec(g.shape), _const_spec(w_in.shape), _const_spec(wout.shape)]
    args += [g, w_in, wout]
    return pl.pallas_call(
        functools.partial(_ffn_kernel, merge=merge_args is not None, mod_base=mod_base),
        grid=(bsz, t // tm),
        in_specs=in_specs,
        out_specs=tile(d),
        out_shape=jax.ShapeDtypeStruct(x.shape, F32),
        scratch_shapes=[pltpu.VMEM((tm, d), BF16), pltpu.VMEM((tm, d), F32)],
        compiler_params=pltpu.CompilerParams(vmem_limit_bytes=VMEM_LIMIT_BYTES),
        name="ffn_merge" if merge_args is not None else "ffn",
    )(*args)


def _rope(xg, cos, sin_next, sin_prev):
    return xg * cos + pltpu.roll(xg, LANES - 8, 1) * sin_next + pltpu.roll(xg, 8, 1) * sin_prev


def _mix_kernel(x_ref, xp_ref, xn_ref, mod_ref, gmix_ref, w_ref, gql_ref, wqt_ref, gkvl_ref, wk_ref, wvt_ref,
                gqc_ref, gks_ref, gkpe_ref, shift_ref, cos_ref, sn_ref, sp_ref, cost_ref, sint_ref,
                gsgu_ref, wsp_ref, bsp_ref,
                wcv_ref, gosg_ref, gocv_ref, q_ref, k_ref, v_ref, sgcv_ref, h_scr, z_scr, *, tm, q_scale):
    i = pl.program_id(1)
    last = pl.num_programs(1) - 1
    halo = BF16_ROWS
    shift = mod_ref[0, 3:4, :]
    scale1 = 1.0 + mod_ref[0, 4:5, :]
    gmix = gmix_ref[...]

    def norm_mod(x):
        return (_rms(x, gmix) * scale1 + shift).astype(BF16)

    h_scr[0:halo] = norm_mod(xp_ref[0])
    h_scr[halo:halo + tm] = norm_mod(x_ref[0])
    h_scr[halo + tm:] = norm_mod(xn_ref[0])
    hm = h_scr[halo:halo + tm]

    cos = cos_ref[...]
    sn = sn_ref[...]
    sp = sp_ref[...]

    q_kpe = _dot(hm, w_ref[:, MIX_Q:MIX_KV])
    qn = _rms(q_kpe[:, :Q_LORA], gql_ref[...]).astype(BF16)
    qst = lax.dot_general(wqt_ref[...], qn, (((1,), (1,)), ((), ())), preferred_element_type=F32)
    cost = cost_ref[...]
    sint = sint_ref[...]
    gq = gqc_ref[...]
    lane = lax.broadcasted_iota(jnp.int32, (1, LANES), 1)
    half = AXIS_DIM // 2
    for h in range(MLA_HEADS):
        xq = qst[h * LANES:(h + 1) * LANES]
        r = lax.rsqrt(jnp.sum(xq * xq, axis=0, keepdims=True) * (1.0 / QK_HEAD_DIM) + EPS) * q_scale
        xg = xq * gq
        pe = xg[NOPE_DIM:QK_HEAD_DIM]
        swapped = jnp.concatenate([pe[half:2 * half], pe[0:half], pe[3 * half:4 * half], pe[2 * half:3 * half]], axis=0)
        y = jnp.concatenate([xg[0:NOPE_DIM], pe * cost + swapped * sint, xg[QK_HEAD_DIM:]], axis=0)
        q_ref[0, h] = (y * r + shift_ref[...]).astype(BF16)

    kvn = _rms(_dot(hm, w_ref[:, MIX_KV:MIX_KV + KV_LORA]), gkvl_ref[...]).astype(BF16)
    kk = _dot(kvn, wk_ref[...])
    vt = lax.dot_general(wvt_ref[...], kvn, (((1,), (1,)), ((), ())), preferred_element_type=F32)
    kpe = q_kpe[:, Q_LORA:]
    ss_pe = jnp.sum(kpe * kpe, axis=-1, keepdims=True)
    pe = _rope(kpe * gkpe_ref[...], cos, sn, sp)
    k_one = jnp.where(lane == SHIFT_LANE, 1.0, 0.0)
    v_one = jnp.where(lax.broadcasted_iota(jnp.int32, (LANES, 1), 0) == V_HEAD_DIM, 1.0, 0.0)
    for h in range(MLA_HEADS):
        sl = slice(h * LANES, (h + 1) * LANES)
        xk = kk[:, sl]
        r = lax.rsqrt((jnp.sum(xk * xk, axis=-1, keepdims=True) + ss_pe) * (1.0 / QK_HEAD_DIM) + EPS)
        k_ref[0, h] = ((xk * gks_ref[:, sl] + pe) * r + k_one).astype(BF16)
        v_ref[0, h] = (vt[sl, :] + v_one).astype(BF16)

    ge = _gelu_tanh(_dot(hm, w_ref[:, MIX_SG:MIX_SG + 2 * SG_WIDTH]))
    u = ge[:, :SG_WIDTH]
    low = lane < SG_GROUP_DIM
    vparts = []
    for c in range(SG_WIDTH // LANES):
        vv = ge[:, SG_WIDTH + c * LANES:SG_WIDTH + (c + 1) * LANES]
        v2 = vv * vv
        s_lo = jnp.sum(jnp.where(low, v2, 0.0), axis=-1, keepdims=True)
        s_hi = jnp.sum(jnp.where(low, 0.0, v2), axis=-1, keepdims=True)
        r = jnp.where(low, lax.rsqrt(s_lo * (1.0 / SG_GROUP_DIM) + EPS), lax.rsqrt(s_hi * (1.0 / SG_GROUP_DIM) + EPS))
        vparts.append(vv * r * gsgu_ref[:, c * LANES:(c + 1) * LANES])
    vn = jnp.concatenate(vparts, axis=-1)
    group = lax.broadcasted_iota(jnp.int32, (1, SG_WIDTH), 1) // SG_GROUP_DIM
    sg_rows = []
    for c in range(tm // CHUNK):
        vc = vn[c * CHUNK:(c + 1) * CHUNK]
        stacked = jnp.concatenate([jnp.where(group == g, vc, 0.0) for g in range(SG_GROUPS)], axis=0)
        vs = _dot(wsp_ref[...], stacked.astype(BF16)) + bsp_ref[...]
        sg_rows.append(u[c * CHUNK:(c + 1) * CHUNK] * vs)
    sg = jnp.concatenate(sg_rows, axis=0)
    sgcv_ref[0, :, 0:SG_WIDTH] = _rms(sg, gosg_ref[...]).astype(BF16)

    cvin = _dot(h_scr[...], w_ref[:, MIX_CV:MIX_CV + 3 * CONV_WIDTH])
    z = cvin[:, CONV_WIDTH:2 * CONV_WIDTH] * cvin[:, 2 * CONV_WIDTH:]
    z_scr[...] = z
    z_scr[0:halo] = jnp.where(i > 0, z[0:halo], 0.0)
    z_scr[halo + tm:] = jnp.where(i < last, z[halo + tm:], 0.0)
    y = (wcv_ref[0:1, :] * z_scr[halo - 1:halo - 1 + tm] + wcv_ref[1:2, :] * z_scr[halo:halo + tm]
         + wcv_ref[2:3, :] * z_scr[halo + 1:halo + 1 + tm])
    cv = cvin[halo:halo + tm, 0:CONV_WIDTH] * y
    sgcv_ref[0, :, SG_WIDTH:] = _rms(cv, gocv_ref[...]).astype(BF16)


def _mix(x, mod, p, tables):
    bsz, t, d = x.shape
    tm = min(TOKEN_TILE, t)
    halo = BF16_ROWS
    nblk = t // halo
    per = tm // halo
    cos, sn, sp, cost, sint = tables
    const_args = [p["g_mix"], p["w_mix"], p["g_q_lat"], p["w_qt"], p["g_kv_lat"], p["w_k"], p["w_vt"],
                  p["g_q_col"], p["g_k_slot"], p["g_k_pe"], p["q_shift"]]
    tail_args = [p["g_sgu"], p["w_sp"], p["b_sp"], p["w_conv"], p["g_out_sg"], p["g_out_cv"]]
    tab_spec = pl.BlockSpec((tm, LANES), lambda b, i: (i, 0))
    tabt_spec = pl.BlockSpec((ROPE_DIM, tm), lambda b, i: (0, i))
    in_specs = ([pl.BlockSpec((1, tm, d), lambda b, i: (b, i, 0)),
                 pl.BlockSpec((1, halo, d), lambda b, i: (b, jnp.maximum(i * per - 1, 0), 0)),
                 pl.BlockSpec((1, halo, d), lambda b, i: (b, jnp.minimum((i + 1) * per, nblk - 1), 0)),
                 pl.BlockSpec((1, ADA_CHUNKS, d), lambda b, i: (b, 0, 0))]
                + [_const_spec(a.shape) for a in const_args]
                + [tab_spec, tab_spec, tab_spec, tabt_spec, tabt_spec]
                + [_const_spec(a.shape) for a in tail_args])
    head_spec = pl.BlockSpec((1, MLA_HEADS, tm, LANES), lambda b, i: (b, 0, i, 0))
    headt_spec = pl.BlockSpec((1, MLA_HEADS, LANES, tm), lambda b, i: (b, 0, 0, i))
    return pl.pallas_call(
        functools.partial(_mix_kernel, tm=tm, q_scale=LOG2E * QK_HEAD_DIM ** -0.5),
        grid=(bsz, t // tm),
        in_specs=in_specs,
        out_specs=[headt_spec, head_spec, headt_spec,
                   pl.BlockSpec((1, tm, SG_WIDTH + CONV_WIDTH), lambda b, i: (b, i, 0))],
        out_shape=[jax.ShapeDtypeStruct((bsz, MLA_HEADS, LANES, t), BF16),
                   jax.ShapeDtypeStruct((bsz, MLA_HEADS, t, LANES), BF16),
                   jax.ShapeDtypeStruct((bsz, MLA_HEADS, LANES, t), BF16),
                   jax.ShapeDtypeStruct((bsz, t, SG_WIDTH + CONV_WIDTH), BF16)],
        scratch_shapes=[pltpu.VMEM((tm + 2 * halo, d), BF16), pltpu.VMEM((tm + 2 * halo, CONV_WIDTH), F32)],
        compiler_params=pltpu.CompilerParams(vmem_limit_bytes=VMEM_LIMIT_BYTES),
        name="mixer_front",
    )(x, x, x, mod, *const_args, cos, sn, sp, cost, sint, *tail_args)


NEG_BIG = -1e30
VT_ROWS = LANES
SCORE_LEAD = 1


def _attn_kernel(*refs, tk, has_extra, online):
    if has_extra:
        q_ref, k_ref, vt_ref, k2_ref, vt2_ref, o_ref = refs
    else:
        q_ref, k_ref, vt_ref, o_ref = refs
    tq = q_ref.shape[3]
    qs = [q_ref[0, h] for h in range(2)]

    def consume(st, vth, state):
        if online:
            m, acc = state
            m_new = jnp.maximum(m, jnp.max(st, axis=0, keepdims=True))
            acc = jnp.exp2(m - m_new) * acc + _dot(vth, jnp.exp2(st - m_new).astype(BF16))
            return m_new, acc
        (acc,) = state
        return (acc + _dot(vth, jnp.exp2(st).astype(BF16)),)

    chunks = []
    for j in range(k_ref.shape[2] // tk):
        ks = slice(j * tk, (j + 1) * tk)
        chunks.append((lambda h, ks=ks: k_ref[0, h, ks, :], lambda h, ks=ks: vt_ref[0, h, 0:VT_ROWS, ks]))
    if has_extra:
        chunks.append((lambda h: k2_ref[0, h], lambda h: vt2_ref[0, h, 0:VT_ROWS, :]))

    acc0 = jnp.zeros((VT_ROWS, tq), F32)
    init = (jnp.full((1, tq), NEG_BIG, F32), acc0) if online else (acc0,)
    states = [init, init]
    n = len(chunks)
    score = lambda j: [_dot(chunks[j][0](h), qs[h]) for h in range(2)]
    pending = [score(j) for j in range(min(SCORE_LEAD, n))]
    for j in range(n):
        if j + SCORE_LEAD < n:
            pending.append(score(j + SCORE_LEAD))
        scores = pending.pop(0)
        states = [consume(scores[h], chunks[j][1](h), states[h]) for h in range(2)]
    outs = []
    for st in states:
        acc = st[-1]
        outs.append(acc[0:V_HEAD_DIM] / acc[V_HEAD_DIM:V_HEAD_DIM + 1])
    o_ref[0] = jnp.concatenate(outs, axis=0).T


def _attention(q, k, v, k2=None, v2=None, *, online):
    bsz, heads, _, t = q.shape
    t1 = k.shape[2]
    tq = min(Q_TILE, t)
    tk = min(KV_TILE, t1)
    has_extra = k2 is not None
    q_spec = pl.BlockSpec((1, 2, LANES, tq), lambda b, p, i: (b, p, 0, i))
    k_spec = lambda n: pl.BlockSpec((1, 2, n, LANES), lambda b, p, i: (b, p, 0, 0))
    vt_spec = lambda n: pl.BlockSpec((1, 2, LANES, n), lambda b, p, i: (b, p, 0, 0))
    in_specs = [q_spec, k_spec(t1), vt_spec(t1)]
    args = [q, k, v]
    if has_extra:
        in_specs += [k_spec(k2.shape[2]), vt_spec(k2.shape[2])]
        args += [k2, v2]
    return pl.pallas_call(
        functools.partial(_attn_kernel, tk=tk, has_extra=has_extra, online=online),
        grid=(bsz, heads // 2, t // tq),
        in_specs=in_specs,
        out_specs=pl.BlockSpec((1, tq, LANES), lambda b, p, i: (b, i, p)),
        out_shape=jax.ShapeDtypeStruct((bsz, t, ATT_WIDTH), F32),
        compiler_params=pltpu.CompilerParams(vmem_limit_bytes=VMEM_LIMIT_BYTES),
        name=("attention_online" if online else "attention_shifted") + ("_extra" if has_extra else ""),
    )(*args)


def _slot_pad(w, width):
    lead = w.shape[:-1]
    w = w.reshape(*lead, MLA_HEADS, width)
    w = jnp.pad(w, [(0, 0)] * len(lead) + [(0, 0), (0, LANES - width)])
    return w.reshape(*lead, MLA_HEADS * LANES)


def _prep_ffn(w_in, w_out):
    return w_in.astype(BF16), w_out.astype(BF16)


def _max_shift(g_q, g_k):
    return (LOG2E * QK_HEAD_DIM ** 0.5) * jnp.max(jnp.abs(g_q)) * jnp.max(jnp.abs(g_k))


def _prep_layer(l, w):
    row = lambda a: a.reshape(1, -1).astype(F32)
    w_mix_in = w["w_mix_in"][l]
    d = w_mix_in.shape[0]
    kpe_cols = jnp.zeros((d, LANES), F32).at[:, NOPE_DIM:QK_HEAD_DIM].set(w_mix_in[:, OFF_KPE:OFF_SG])
    w_mix = jnp.concatenate([w_mix_in[:, :OFF_KV], kpe_cols, w_mix_in[:, OFF_KV:OFF_KPE], w_mix_in[:, OFF_SG:]],
                            axis=1).astype(BF16)
    w_kv = w["w_kv_up"][l].reshape(KV_LORA, MLA_HEADS, NOPE_DIM + V_HEAD_DIM)
    w_k = _slot_pad(w_kv[:, :, :NOPE_DIM].reshape(KV_LORA, -1), NOPE_DIM)
    w_v = _slot_pad(w_kv[:, :, NOPE_DIM:].reshape(KV_LORA, ATT_WIDTH), V_HEAD_DIM)
    g_k = w["g_k_head"][l]
    g_out = w["g_out"][l]
    w_sp = w["w_spatial"][l]
    ffn1 = _prep_ffn(w["w_ffn1_in"][l], w["w_ffn1_out"][l])
    ffn2 = _prep_ffn(w["w_ffn2_in"][l], w["w_ffn2_out"][l])
    return {
        "g_ffn1": row(w["g_ffn1"][l]), "ffn1_ab": ffn1[0], "ffn1_out": ffn1[1],
        "g_ffn2": row(w["g_ffn2"][l]), "ffn2_ab": ffn2[0], "ffn2_out": ffn2[1],
        "g_mix": row(w["g_mix"][l]), "w_mix": w_mix,
        "g_q_lat": row(w["g_q_lat"][l]), "w_qt": _slot_pad(w["w_q_up"][l], QK_HEAD_DIM).T.astype(BF16),
        "g_kv_lat": row(w["g_kv_lat"][l]), "w_k": w_k.astype(BF16), "w_vt": w_v.T.astype(BF16),
        "g_q_col": jnp.pad(w["g_q_head"][l].astype(F32), (0, LANES - QK_HEAD_DIM)).reshape(LANES, 1),
        "g_k_slot": row(_slot_pad(jnp.tile(g_k[:NOPE_DIM], MLA_HEADS), NOPE_DIM)),
        "g_k_pe": row(jnp.zeros((LANES,), F32).at[NOPE_DIM:QK_HEAD_DIM].set(g_k[NOPE_DIM:])),
        "q_shift": jnp.zeros((LANES, 1), F32).at[SHIFT_LANE, 0].set(-_max_shift(w["g_q_head"][l], g_k)),
        "g_sgu": row(w["g_sgu"][l]),
        "w_sp": jnp.concatenate([w_sp[g] for g in range(SG_GROUPS)], axis=1).astype(BF16),
        "b_sp": jnp.repeat(w["b_spatial"][l].T, SG_GROUP_DIM, axis=1).astype(F32),
        "w_conv": w["w_conv"][l].astype(F32),
        "g_out_att": row(g_out[:ATT_WIDTH]),
        "g_out_sg": row(g_out[ATT_WIDTH:ATT_WIDTH + SG_WIDTH]),
        "g_out_cv": row(g_out[ATT_WIDTH + SG_WIDTH:]),
        "w_mix_out": w["w_mix_out"][l].astype(BF16),
    }


def _rope_tables(n):
    rows = n // GRID_W
    row = jnp.repeat(jnp.arange(rows), GRID_W).astype(F32)
    col = jnp.tile(jnp.arange(GRID_W), rows).astype(F32)
    inv = 1.0 / (ROPE_BASE ** (jnp.arange(0, AXIS_DIM, 2, dtype=F32) / AXIS_DIM))
    ang_r = row[:, None] * inv
    ang_c = col[:, None] * inv
    ang = jnp.concatenate([ang_r, ang_r, ang_c, ang_c], axis=-1)
    cos, sin = jnp.cos(ang), jnp.sin(ang)
    first_half = (jnp.arange(ROPE_DIM) % AXIS_DIM) < AXIS_DIM // 2
    pad = lambda a, fill: jnp.pad(a, ((0, 0), (NOPE_DIM, LANES - QK_HEAD_DIM)), constant_values=fill)
    return (pad(cos, 1.0), pad(jnp.where(first_half, -sin, 0.0), 0.0), pad(jnp.where(first_half, 0.0, sin), 0.0),
            cos.T, jnp.where(first_half, -sin, sin).T)


def _identity_tables(n):
    return (jnp.ones((n, LANES), F32), jnp.zeros((n, LANES), F32), jnp.zeros((n, LANES), F32),
            jnp.ones((ROPE_DIM, n), F32), jnp.zeros((ROPE_DIM, n), F32))


def kernel(x, c, ctx, c_ctx, w_ada, b_ada, g_ffn1, w_ffn1_in, w_ffn1_out, g_mix, w_mix_in, g_q_lat, w_q_up, g_kv_lat, w_kv_up, g_q_head, g_k_head, g_sgu, w_spatial, b_spatial, w_conv, g_out, w_mix_out, g_ffn2, w_ffn2_in, w_ffn2_out):
    weights = dict(g_ffn1=g_ffn1, w_ffn1_in=w_ffn1_in, w_ffn1_out=w_ffn1_out, g_mix=g_mix, w_mix_in=w_mix_in,
                   g_q_lat=g_q_lat, w_q_up=w_q_up, g_kv_lat=g_kv_lat, w_kv_up=w_kv_up, g_q_head=g_q_head,
                   g_k_head=g_k_head, g_sgu=g_sgu, w_spatial=w_spatial, b_spatial=b_spatial, w_conv=w_conv,
                   g_out=g_out, w_mix_out=w_mix_out, g_ffn2=g_ffn2, w_ffn2_in=w_ffn2_in, w_ffn2_out=w_ffn2_out)
    bsz, t, d = x.shape
    depth = w_ada.shape[0]
    rows = 8 * ((bsz + 1 + 7) // 8)
    cvec = jnp.zeros((rows, d), F32).at[:bsz].set(c).at[bsz].set(c_ctx)
    mod = _ada(cvec, w_ada, b_ada)
    lat_tables = _rope_tables(t)
    ctx_tables = _identity_tables(ctx.shape[1])

    h, hc = x, ctx
    for l in range(depth):
        last = l == depth - 1
        p = _prep_layer(l, weights)
        mod_l = mod[l, :bsz].reshape(bsz, ADA_CHUNKS, d)
        mod_c1 = mod[l, bsz].reshape(1, ADA_CHUNKS, d)
        mod_c = jnp.broadcast_to(mod_c1, (bsz, ADA_CHUNKS, d))
        flat = lambda a: a.reshape(1, -1, a.shape[-1])

        h = _ffn(h, mod_l, p["g_ffn1"], p["ffn1_ab"], p["ffn1_out"], mod_base=0)
        hc = _ffn(flat(hc), mod_c1, p["g_ffn1"], p["ffn1_ab"], p["ffn1_out"], mod_base=0).reshape(ctx.shape)

        q, k, v, sgcv = _mix(h, mod_l, p, lat_tables)
        qc, kc, vc, sgcv_c = _mix(hc, mod_c, p, ctx_tables)
        attn = lax.cond(_max_shift(g_q_head[l], g_k_head[l]) <= MAX_SAFE_SHIFT,
                        functools.partial(_attention, online=False),
                        functools.partial(_attention, online=True), q, k, v, kc, vc)
        h = _ffn(h, mod_l, p["g_ffn2"], p["ffn2_ab"], p["ffn2_out"], mod_base=6,
                 merge_args=(attn, sgcv, p["g_out_att"], p["w_mix_out"]))
        if not last:
            attn_c = _attention(qc, kc, vc, online=True)
            hc = _ffn(flat(hc), mod_c1, p["g_ffn2"], p["ffn2_ab"], p["ffn2_out"], mod_base=6,
                      merge_args=(flat(attn_c), flat(sgcv_c), p["g_out_att"], p["w_mix_out"])).reshape(ctx.shape)
    return h
```

```python
import functools

import jax
import jax.numpy as jnp
from jax import lax
from jax.experimental import pallas as pl
from jax.experimental.pallas import tpu as pltpu

EPS = 1e-6
ADA_CHUNKS = 9
FFN_RESIDUAL_WEIGHT = 0.5
GRID_W = 64
ROPE_BASE = 10000.0
MLA_HEADS = 8
NOPE_DIM = 64
ROPE_DIM = 32
AXIS_DIM = ROPE_DIM // 2
QK_HEAD_DIM = NOPE_DIM + ROPE_DIM
V_HEAD_DIM = 64
Q_LORA = 384
KV_LORA = 256
ATT_WIDTH = MLA_HEADS * V_HEAD_DIM
SG_GROUPS = 4
SG_GROUP_DIM = 64
SG_WIDTH = SG_GROUPS * SG_GROUP_DIM
CHUNK = 128
CONV_WIDTH = 256
OFF_KV = Q_LORA
OFF_KPE = OFF_KV + KV_LORA
OFF_SG = OFF_KPE + ROPE_DIM
OFF_CONV = OFF_SG + 2 * SG_WIDTH
LOG2E = 1.4426950408889634
SHIFT_LANE = QK_HEAD_DIM
MAX_SAFE_SHIFT = 60.0

LANES = 128
BF16_ROWS = 16
VMEM_LIMIT_BYTES = 56 * 1024 * 1024

MIX_Q = 0
MIX_KPE = MIX_Q + Q_LORA
MIX_KV = MIX_KPE + LANES
MIX_SG = MIX_KV + KV_LORA
MIX_CV = MIX_SG + 2 * SG_WIDTH
MIX_COLS = MIX_CV + 3 * CONV_WIDTH

FF_CHUNK = 256
TOKEN_TILE = 512
FFN_TILE = 1024
FFN_SUBTILE = 512
Q_TILE = 512
KV_TILE = 256

BF16 = jnp.bfloat16
F32 = jnp.float32


def _dot(a, b):
    return jnp.dot(a, b, preferred_element_type=F32)


def _rms(x, g):
    return x * lax.rsqrt(jnp.mean(x * x, axis=-1, keepdims=True) + EPS) * g


def _silu(x):
    return x * jax.nn.sigmoid(x)


def _gelu_tanh(x):
    return 0.5 * x * (1.0 + jnp.tanh(0.7978845608028654 * (x + 0.044715 * (x * x * x))))


def _const_spec(shape):
    zeros = (0,) * len(shape)
    return pl.BlockSpec(shape, lambda *_: zeros, pipeline_mode=pl.Buffered(1))


def _ada_kernel(c_ref, w_ref, b_ref, o_ref):
    s = _silu(c_ref[...])
    w = w_ref[0]
    s_hi = s.astype(BF16)
    s_lo = (s - s_hi.astype(F32)).astype(BF16)
    w_hi = w.astype(BF16)
    w_lo = (w - w_hi.astype(F32)).astype(BF16)
    o_ref[0] = _dot(s_hi, w_hi) + _dot(s_lo, w_hi) + _dot(s_hi, w_lo) + b_ref[0]


def _ada(cvec, w_ada, b_ada):
    depth, d, n = w_ada.shape
    rows = cvec.shape[0]
    tn = d
    return pl.pallas_call(
        _ada_kernel,
        grid=(depth, n // tn),
        in_specs=[
            pl.BlockSpec((rows, d), lambda l, j: (0, 0)),
            pl.BlockSpec((1, d, tn), lambda l, j: (l, 0, j)),
            pl.BlockSpec((1, 1, tn), lambda l, j: (l, 0, j)),
        ],
        out_specs=pl.BlockSpec((1, rows, tn), lambda l, j: (l, 0, j)),
        out_shape=jax.ShapeDtypeStruct((depth, rows, n), F32),
        compiler_params=pltpu.CompilerParams(vmem_limit_bytes=VMEM_LIMIT_BYTES),
        name="ada_modulation",
    )(cvec, w_ada, b_ada.reshape(depth, 1, n))


def _ffn_kernel(*refs, merge, mod_base):
    if merge:
        (x_ref, mod_ref, attn_ref, sgcv_ref, gatt_ref, wmo_ref, g_ref, win_ref, wout_ref,
         o_ref, h_scr, acc_scr) = refs
    else:
        x_ref, mod_ref, g_ref, win_ref, wout_ref, o_ref, h_scr, acc_scr = refs
    d_ff = wout_ref.shape[0]
    tm = x_ref.shape[1]
    sub = min(FFN_SUBTILE, tm)
    subs = [slice(r, r + sub) for r in range(0, tm, sub)]
    shift = mod_ref[0, mod_base:mod_base + 1, :]
    scale = mod_ref[0, mod_base + 1:mod_base + 2, :]
    for rows in subs:
        x = x_ref[0, rows, :]
        if merge:
            att = _rms(attn_ref[0, rows, :], gatt_ref[...]).astype(BF16)
            y = _dot(att, wmo_ref[0:ATT_WIDTH, :]) + _dot(sgcv_ref[0, rows, :], wmo_ref[ATT_WIDTH:, :])
            x = x + mod_ref[0, 5:6, :] * y
        o_ref[0, rows, :] = x
        h_scr[rows, :] = (_rms(x, g_ref[...]) * (1.0 + scale) + shift).astype(BF16)

    def chunk(c, rows):
        a = _dot(h_scr[rows, :], win_ref[:, c:c + FF_CHUNK])
        b = _dot(h_scr[rows, :], win_ref[:, d_ff + c:d_ff + c + FF_CHUNK])
        return _dot((_silu(a) * b).astype(BF16), wout_ref[c:c + FF_CHUNK, :])

    for c in range(0, d_ff, FF_CHUNK):
        for rows in subs:
            if c == 0:
                acc_scr[rows, :] = chunk(c, rows)
            else:
                acc_scr[rows, :] += chunk(c, rows)
    gate = mod_ref[0, mod_base + 2:mod_base + 3, :]
    for rows in subs:
        o_ref[0, rows, :] = o_ref[0, rows, :] + (FFN_RESIDUAL_WEIGHT * gate) * acc_scr[rows, :]


def _ffn(x, mod, g, w_in, wout, *, layer, mod_base, merge_args=None):
    bsz, t, d = x.shape
    tm = min(FFN_TILE, t)
    assert wout.shape[1] % FF_CHUNK == 0
    slab = lambda w: pl.BlockSpec((None,) + w.shape[1:], lambda b, i: (layer, 0, 0), pipeline_mode=pl.Buffered(1))
    tile = lambda w: pl.BlockSpec((1, tm, w), lambda b, i: (b, i, 0))
    in_specs = [tile(d), pl.BlockSpec((1, ADA_CHUNKS, d), lambda b, i: (b, 0, 0))]
    args = [x, mod]
    if merge_args is not None:
        attn, sgcv, g_att, wmo = merge_args
        in_specs += [tile(ATT_WIDTH), tile(SG_WIDTH + CONV_WIDTH), _const_spec(g_att.shape), _const_spec(wmo.shape)]
        args += [attn, sgcv, g_att, wmo]
    in_specs += [_const_spec(g.shape), slab(w_in), slab(wout)]
    args += [g, w_in, wout]
    return pl.pallas_call(
        functools.partial(_ffn_kernel, merge=merge_args is not None, mod_base=mod_base),
        grid=(bsz, t // tm),
        in_specs=in_specs,
        out_specs=tile(d),
        out_shape=jax.ShapeDtypeStruct(x.shape, F32),
        scratch_shapes=[pltpu.VMEM((tm, d), BF16), pltpu.VMEM((tm, d), F32)],
        compiler_params=pltpu.CompilerParams(vmem_limit_bytes=VMEM_LIMIT_BYTES),
        name="ffn_merge" if merge_args is not None else "ffn",
    )(*args)


def _rope(xg, cos, sin_next, sin_prev):
    return xg * cos + pltpu.roll(xg, LANES - 8, 1) * sin_next + pltpu.roll(xg, 8, 1) * sin_prev


def _mix_kernel(x_ref, xp_ref, xn_ref, mod_ref, gmix_ref, w_ref, gql_ref, wqt_ref, gkvl_ref, wk_ref, wvt_ref,
                gqc_ref, gks_ref, gkpe_ref, shift_ref, cos_ref, sn_ref, sp_ref, cost_ref, sint_ref,
                gsgu_ref, wsp_ref, bsp_ref,
                wcv_ref, gosg_ref, gocv_ref, q_ref, k_ref, v_ref, sgcv_ref, h_scr, z_scr, *, tm, q_scale):
    i = pl.program_id(1)
    last = pl.num_programs(1) - 1
    halo = BF16_ROWS
    shift = mod_ref[0, 3:4, :]
    scale1 = 1.0 + mod_ref[0, 4:5, :]
    gmix = gmix_ref[...]

    def norm_mod(x):
        return (_rms(x, gmix) * scale1 + shift).astype(BF16)

    h_scr[0:halo] = norm_mod(xp_ref[0])
    h_scr[halo:halo + tm] = norm_mod(x_ref[0])
    h_scr[halo + tm:] = norm_mod(xn_ref[0])
    hm = h_scr[halo:halo + tm]

    cos = cos_ref[...]
    sn = sn_ref[...]
    sp = sp_ref[...]

    q_kpe = _dot(hm, w_ref[:, MIX_Q:MIX_KV])
    kvlat = _dot(hm, w_ref[:, MIX_KV:MIX_KV + KV_LORA])
    sgin = _dot(hm, w_ref[:, MIX_SG:MIX_SG + 2 * SG_WIDTH])
    cvin = _dot(h_scr[...], w_ref[:, MIX_CV:MIX_CV + 3 * CONV_WIDTH])

    qn = _rms(q_kpe[:, :Q_LORA], gql_ref[...]).astype(BF16)
    kvn = _rms(kvlat, gkvl_ref[...]).astype(BF16)
    qst = lax.dot_general(wqt_ref[...], qn, (((1,), (1,)), ((), ())), preferred_element_type=F32)
    kk = _dot(kvn, wk_ref[...])
    vt = lax.dot_general(wvt_ref[...], kvn, (((1,), (1,)), ((), ())), preferred_element_type=F32)
    cost = cost_ref[...]
    sint = sint_ref[...]
    gq = gqc_ref[...]
    lane = lax.broadcasted_iota(jnp.int32, (1, LANES), 1)
    half = AXIS_DIM // 2
    for h in range(MLA_HEADS):
        xq = qst[h * LANES:(h + 1) * LANES]
        r = lax.rsqrt(jnp.sum(xq * xq, axis=0, keepdims=True) * (1.0 / QK_HEAD_DIM) + EPS) * q_scale
        xg = xq * gq
        pe = xg[NOPE_DIM:QK_HEAD_DIM]
        swapped = jnp.concatenate([pe[half:2 * half], pe[0:half], pe[3 * half:4 * half], pe[2 * half:3 * half]], axis=0)
        y = jnp.concatenate([xg[0:NOPE_DIM], pe * cost + swapped * sint, xg[QK_HEAD_DIM:]], axis=0)
        q_ref[0, h] = (y * r + shift_ref[...]).astype(BF16)

    kpe = q_kpe[:, Q_LORA:]
    ss_pe = jnp.sum(kpe * kpe, axis=-1, keepdims=True)
    pe = _rope(kpe * gkpe_ref[...], cos, sn, sp)
    k_one = jnp.where(lane == SHIFT_LANE, 1.0, 0.0)
    v_one = jnp.where(lax.broadcasted_iota(jnp.int32, (LANES, 1), 0) == V_HEAD_DIM, 1.0, 0.0)
    for h in range(MLA_HEADS):
        sl = slice(h * LANES, (h + 1) * LANES)
        xk = kk[:, sl]
        r = lax.rsqrt((jnp.sum(xk * xk, axis=-1, keepdims=True) + ss_pe) * (1.0 / QK_HEAD_DIM) + EPS)
        k_ref[0, h] = ((xk * gks_ref[:, sl] + pe) * r + k_one).astype(BF16)
        v_ref[0, h] = (vt[sl, :] + v_one).astype(BF16)

    ge = _gelu_tanh(sgin)
    u = ge[:, :SG_WIDTH]
    low = lane < SG_GROUP_DIM
    vparts = []
    for c in range(SG_WIDTH // LANES):
        vv = ge[:, SG_WIDTH + c * LANES:SG_WIDTH + (c + 1) * LANES]
        v2 = vv * vv
        s_lo = jnp.sum(jnp.where(low, v2, 0.0), axis=-1, keepdims=True)
        s_hi = jnp.sum(jnp.where(low, 0.0, v2), axis=-1, keepdims=True)
        r = jnp.where(low, lax.rsqrt(s_lo * (1.0 / SG_GROUP_DIM) + EPS), lax.rsqrt(s_hi * (1.0 / SG_GROUP_DIM) + EPS))
        vparts.append(vv * r * gsgu_ref[:, c * LANES:(c + 1) * LANES])
    vn = jnp.concatenate(vparts, axis=-1)
    group = lax.broadcasted_iota(jnp.int32, (1, SG_WIDTH), 1) // SG_GROUP_DIM
    sg_rows = []
    for c in range(tm // CHUNK):
        vc = vn[c * CHUNK:(c + 1) * CHUNK]
        stacked = jnp.concatenate([jnp.where(group == g, vc, 0.0) for g in range(SG_GROUPS)], axis=0)
        vs = _dot(wsp_ref[...], stacked.astype(BF16)) + bsp_ref[...]
        sg_rows.append(u[c * CHUNK:(c + 1) * CHUNK] * vs)
    sg = jnp.concatenate(sg_rows, axis=0)
    sgcv_ref[0, :, 0:SG_WIDTH] = _rms(sg, gosg_ref[...]).astype(BF16)

    z =cvin[:, CONV_WIDTH:2 * CONV_WIDTH] * cvin[:, 2 * CONV_WIDTH:]
    z_scr[...] = z
    z_scr[0:halo] = jnp.where(i > 0, z[0:halo], 0.0)
    z_scr[halo + tm:] = jnp.where(i < last, z[halo + tm:], 0.0)
    y = (wcv_ref[0:1, :] * z_scr[halo - 1:halo - 1 + tm] + wcv_ref[1:2, :] * z_scr[halo:halo + tm]
         + wcv_ref[2:3, :] * z_scr[halo + 1:halo + 1 + tm])
    cv = cvin[halo:halo + tm, 0:CONV_WIDTH] * y
    sgcv_ref[0, :, SG_WIDTH:] = _rms(cv, gocv_ref[...]).astype(BF16)


def _mix(x, mod, p, tables):
    bsz, t, d = x.shape
    tm = min(TOKEN_TILE, t)
    halo = BF16_ROWS
    nblk = t // halo
    per = tm // halo
    cos, sn, sp, cost, sint = tables
    const_args = [p["g_mix"], p["w_mix"], p["g_q_lat"], p["w_qt"], p["g_kv_lat"], p["w_k"], p["w_vt"],
                  p["g_q_col"], p["g_k_slot"], p["g_k_pe"], p["q_shift"]]
    tail_args = [p["g_sgu"], p["w_sp"], p["b_sp"], p["w_conv"], p["g_out_sg"], p["g_out_cv"]]
    tab_spec = pl.BlockSpec((tm, LANES), lambda b, i: (i, 0))
    tabt_spec = pl.BlockSpec((ROPE_DIM, tm), lambda b, i: (0, i))
    in_specs = ([pl.BlockSpec((1, tm, d), lambda b, i: (b, i, 0)),
                 pl.BlockSpec((1, halo, d), lambda b, i: (b, jnp.maximum(i * per - 1, 0), 0)),
                 pl.BlockSpec((1, halo, d), lambda b, i: (b, jnp.minimum((i + 1) * per, nblk - 1), 0)),
                 pl.BlockSpec((1, ADA_CHUNKS, d), lambda b, i: (b, 0, 0))]
                + [_const_spec(a.shape) for a in const_args]
                + [tab_spec, tab_spec, tab_spec, tabt_spec, tabt_spec]
                + [_const_spec(a.shape) for a in tail_args])
    head_spec = pl.BlockSpec((1, MLA_HEADS, tm, LANES), lambda b, i: (b, 0, i, 0))
    headt_spec = pl.BlockSpec((1, MLA_HEADS, LANES, tm), lambda b, i: (b, 0, 0, i))
    return pl.pallas_call(
        functools.partial(_mix_kernel, tm=tm, q_scale=LOG2E * QK_HEAD_DIM ** -0.5),
        grid=(bsz, t // tm),
        in_specs=in_specs,
        out_specs=[headt_spec, head_spec, headt_spec,
                   pl.BlockSpec((1, tm, SG_WIDTH + CONV_WIDTH), lambda b, i: (b, i, 0))],
        out_shape=[jax.ShapeDtypeStruct((bsz, MLA_HEADS, LANES, t), BF16),
                   jax.ShapeDtypeStruct((bsz, MLA_HEADS, t, LANES), BF16),
                   jax.ShapeDtypeStruct((bsz, MLA_HEADS, LANES, t), BF16),
                   jax.ShapeDtypeStruct((bsz, t, SG_WIDTH + CONV_WIDTH), BF16)],
        scratch_shapes=[pltpu.VMEM((tm + 2 * halo, d), BF16), pltpu.VMEM((tm + 2 * halo, CONV_WIDTH), F32)],
        compiler_params=pltpu.CompilerParams(vmem_limit_bytes=VMEM_LIMIT_BYTES),
        name="mixer_front",
    )(x, x, x, mod, *const_args, cos, sn, sp, cost, sint, *tail_args)


NEG_BIG = -1e30
VT_ROWS = LANES
SCORE_LEAD = 1


def _attn_kernel(*refs, tk, has_extra, online):
    if has_extra:
        q_ref, k_ref, vt_ref, k2_ref, vt2_ref, o_ref = refs
    else:
        q_ref, k_ref, vt_ref, o_ref = refs
    tq = q_ref.shape[3]
    qs = [q_ref[0, h] for h in range(2)]

    def consume(st, vth, state):
        if online:
            m, acc = state
            m_new = jnp.maximum(m, jnp.max(st, axis=0, keepdims=True))
            acc = jnp.exp2(m - m_new) * acc + _dot(vth, jnp.exp2(st - m_new).astype(BF16))
            return m_new, acc
        (acc,) = state
        return (acc + _dot(vth, jnp.exp2(st).astype(BF16)),)

    chunks = []
    for j in range(k_ref.shape[2] // tk):
        ks = slice(j * tk, (j + 1) * tk)
        chunks.append((lambda h, ks=ks: k_ref[0, h, ks, :], lambda h, ks=ks: vt_ref[0, h, 0:VT_ROWS, ks]))
    if has_extra:
        chunks.append((lambda h: k2_ref[0, h], lambda h: vt2_ref[0, h, 0:VT_ROWS, :]))

    acc0 = jnp.zeros((VT_ROWS, tq), F32)
    init = (jnp.full((1, tq), NEG_BIG, F32), acc0) if online else (acc0,)
    states = [init, init]
    n = len(chunks)
    score = lambda j: [_dot(chunks[j][0](h), qs[h]) for h in range(2)]
    pending = [score(j) for j in range(min(SCORE_LEAD, n))]
    for j in range(n):
        if j + SCORE_LEAD < n:
            pending.append(score(j + SCORE_LEAD))
        scores = pending.pop(0)
        states = [consume(scores[h], chunks[j][1](h), states[h]) for h in range(2)]
    outs = []
    for st in states:
        acc = st[-1]
        outs.append(acc[0:V_HEAD_DIM] / acc[V_HEAD_DIM:V_HEAD_DIM + 1])
    o_ref[0] = jnp.concatenate(outs, axis=0).T


def _attention(q, k, v, k2=None, v2=None, *, online):
    bsz, heads, _, t = q.shape
    t1 = k.shape[2]
    tq = min(Q_TILE, t)
    tk = min(KV_TILE, t1)
    has_extra = k2 is not None
    q_spec = pl.BlockSpec((1, 2, LANES, tq), lambda b, p, i: (b, p, 0, i))
    k_spec = lambda n: pl.BlockSpec((1, 2, n, LANES), lambda b, p, i: (b, p, 0, 0))
    vt_spec = lambda n: pl.BlockSpec((1, 2, LANES, n), lambda b, p, i: (b, p, 0, 0))
    in_specs = [q_spec, k_spec(t1), vt_spec(t1)]
    args = [q, k, v]
    if has_extra:
        in_specs += [k_spec(k2.shape[2]), vt_spec(k2.shape[2])]
        args += [k2, v2]
    return pl.pallas_call(
        functools.partial(_attn_kernel, tk=tk, has_extra=has_extra, online=online),
        grid=(bsz, heads // 2, t // tq),
        in_specs=in_specs,
        out_specs=pl.BlockSpec((1, tq, LANES), lambda b, p, i: (b, i, p)),
        out_shape=jax.ShapeDtypeStruct((bsz, t, ATT_WIDTH), F32),
        compiler_params=pltpu.CompilerParams(vmem_limit_bytes=VMEM_LIMIT_BYTES),
        name=("attention_online" if online else "attention_shifted") + ("_extra" if has_extra else ""),
    )(*args)


def _slot_pad(w, width):
    lead = w.shape[:-1]
    w = w.reshape(*lead, MLA_HEADS, width)
    w = jnp.pad(w, [(0, 0)] * len(lead) + [(0, 0), (0, LANES - width)])
    return w.reshape(*lead, MLA_HEADS * LANES)


CAST_BLOCK_BYTES = 4 * 1024 * 1024


def _cast_kernel(x_ref, o_ref):
    o_ref[...] = x_ref[...].astype(BF16)


def _cast_bf16(w):
    depth, r, c = w.shape
    rows = [tr for tr in range(BF16_ROWS, r + 1, BF16_ROWS) if r % tr == 0 and tr * c * 4 <= CAST_BLOCK_BYTES]
    tr = max(rows)
    spec = pl.BlockSpec((1, tr, c), lambda l, j: (l, j, 0))
    return pl.pallas_call(
        _cast_kernel, grid=(depth, r // tr), in_specs=[spec], out_specs=spec,
        out_shape=jax.ShapeDtypeStruct(w.shape, BF16),
        compiler_params=pltpu.CompilerParams(vmem_limit_bytes=VMEM_LIMIT_BYTES),
        name="cast_bf16",
    )(w)


def _max_shift(g_q, g_k):
    return (LOG2E * QK_HEAD_DIM ** 0.5) * jnp.max(jnp.abs(g_q)) * jnp.max(jnp.abs(g_k))


def _prep_layer(l, w):
    row = lambda a: a.reshape(1, -1).astype(F32)
    w_mix_in = w["w_mix_in"][l]
    d = w_mix_in.shape[0]
    kpe_cols = jnp.zeros((d, LANES), F32).at[:, NOPE_DIM:QK_HEAD_DIM].set(w_mix_in[:, OFF_KPE:OFF_SG])
    w_mix = jnp.concatenate([w_mix_in[:, :OFF_KV], kpe_cols, w_mix_in[:, OFF_KV:OFF_KPE], w_mix_in[:, OFF_SG:]],
                            axis=1).astype(BF16)
    w_kv = w["w_kv_up"][l].reshape(KV_LORA, MLA_HEADS, NOPE_DIM + V_HEAD_DIM)
    w_k = _slot_pad(w_kv[:, :, :NOPE_DIM].reshape(KV_LORA, -1), NOPE_DIM)
    w_v = _slot_pad(w_kv[:, :, NOPE_DIM:].reshape(KV_LORA, ATT_WIDTH), V_HEAD_DIM)
    g_k = w["g_k_head"][l]
    g_out = w["g_out"][l]
    w_sp = w["w_spatial"][l]
    return {
        "g_ffn1": row(w["g_ffn1"][l]), "g_ffn2": row(w["g_ffn2"][l]),
        "g_mix": row(w["g_mix"][l]), "w_mix": w_mix,
        "g_q_lat": row(w["g_q_lat"][l]), "w_qt": _slot_pad(w["w_q_up"][l], QK_HEAD_DIM).T.astype(BF16),
        "g_kv_lat": row(w["g_kv_lat"][l]), "w_k": w_k.astype(BF16), "w_vt": w_v.T.astype(BF16),
        "g_q_col": jnp.pad(w["g_q_head"][l].astype(F32), (0, LANES - QK_HEAD_DIM)).reshape(LANES, 1),
        "g_k_slot": row(_slot_pad(jnp.tile(g_k[:NOPE_DIM], MLA_HEADS), NOPE_DIM)),
        "g_k_pe": row(jnp.zeros((LANES,), F32).at[NOPE_DIM:QK_HEAD_DIM].set(g_k[NOPE_DIM:])),
        "q_shift": jnp.zeros((LANES, 1), F32).at[SHIFT_LANE, 0].set(-_max_shift(w["g_q_head"][l], g_k)),
        "g_sgu": row(w["g_sgu"][l]),
        "w_sp": jnp.concatenate([w_sp[g] for g in range(SG_GROUPS)], axis=1).astype(BF16),
        "b_sp": jnp.repeat(w["b_spatial"][l].T, SG_GROUP_DIM, axis=1).astype(F32),
        "w_conv": w["w_conv"][l].astype(F32),
        "g_out_att": row(g_out[:ATT_WIDTH]),
        "g_out_sg": row(g_out[ATT_WIDTH:ATT_WIDTH + SG_WIDTH]),
        "g_out_cv": row(g_out[ATT_WIDTH + SG_WIDTH:]),
        "w_mix_out": w["w_mix_out"][l].astype(BF16),
    }


def _rope_tables(n):
    rows = n // GRID_W
    row = jnp.repeat(jnp.arange(rows), GRID_W).astype(F32)
    col = jnp.tile(jnp.arange(GRID_W), rows).astype(F32)
    inv = 1.0 / (ROPE_BASE ** (jnp.arange(0, AXIS_DIM, 2, dtype=F32) / AXIS_DIM))
    ang_r = row[:, None] * inv
    ang_c = col[:, None] * inv
    ang = jnp.concatenate([ang_r, ang_r, ang_c, ang_c], axis=-1)
    cos, sin = jnp.cos(ang), jnp.sin(ang)
    first_half = (jnp.arange(ROPE_DIM) % AXIS_DIM) < AXIS_DIM // 2
    pad = lambda a, fill: jnp.pad(a, ((0, 0), (NOPE_DIM, LANES - QK_HEAD_DIM)), constant_values=fill)
    return (pad(cos, 1.0), pad(jnp.where(first_half, -sin, 0.0), 0.0), pad(jnp.where(first_half, 0.0, sin), 0.0),
            cos.T, jnp.where(first_half, -sin, sin).T)


def _identity_tables(n):
    return (jnp.ones((n, LANES), F32), jnp.zeros((n, LANES), F32), jnp.zeros((n, LANES), F32),
            jnp.ones((ROPE_DIM, n), F32), jnp.zeros((ROPE_DIM, n), F32))


def kernel(x, c, ctx, c_ctx, w_ada, b_ada, g_ffn1, w_ffn1_in, w_ffn1_out, g_mix, w_mix_in, g_q_lat, w_q_up, g_kv_lat, w_kv_up, g_q_head, g_k_head, g_sgu, w_spatial, b_spatial, w_conv, g_out, w_mix_out, g_ffn2, w_ffn2_in, w_ffn2_out):
    weights = dict(g_ffn1=g_ffn1, w_ffn1_in=w_ffn1_in, w_ffn1_out=w_ffn1_out, g_mix=g_mix, w_mix_in=w_mix_in,
                   g_q_lat=g_q_lat, w_q_up=w_q_up, g_kv_lat=g_kv_lat, w_kv_up=w_kv_up, g_q_head=g_q_head,
                   g_k_head=g_k_head, g_sgu=g_sgu, w_spatial=w_spatial, b_spatial=b_spatial, w_conv=w_conv,
                   g_out=g_out, w_mix_out=w_mix_out, g_ffn2=g_ffn2, w_ffn2_in=w_ffn2_in, w_ffn2_out=w_ffn2_out)
    bsz, t, d = x.shape
    depth = w_ada.shape[0]
    rows = 8 * ((bsz + 1 + 7) // 8)
    cvec = jnp.zeros((rows, d), F32).at[:bsz].set(c).at[bsz].set(c_ctx)
    mod = _ada(cvec, w_ada, b_ada)
    lat_tables = _rope_tables(t)
    ctx_tables = _identity_tables(ctx.shape[1])

    h, hc = x, ctx
    ffn1_in, ffn1_out = _cast_bf16(w_ffn1_in), _cast_bf16(w_ffn1_out)
    ffn2_in, ffn2_out = _cast_bf16(w_ffn2_in), _cast_bf16(w_ffn2_out)
    for l in range(depth):
        last = l == depth - 1
        p = _prep_layer(l, weights)
        mod_l = mod[l, :bsz].reshape(bsz, ADA_CHUNKS, d)
        mod_c1 = mod[l, bsz].reshape(1, ADA_CHUNKS, d)
        mod_c = jnp.broadcast_to(mod_c1, (bsz, ADA_CHUNKS, d))
        flat = lambda a: a.reshape(1, -1, a.shape[-1])

        ffn1 = functools.partial(_ffn, g=p["g_ffn1"], w_in=ffn1_in, wout=ffn1_out, layer=l, mod_base=0)
        ffn2 = functools.partial(_ffn, g=p["g_ffn2"], w_in=ffn2_in, wout=ffn2_out, layer=l, mod_base=6)

        h = ffn1(h, mod_l)
        hc = ffn1(flat(hc), mod_c1).reshape(ctx.shape)

        q, k, v, sgcv = _mix(h, mod_l, p, lat_tables)
        qc, kc, vc, sgcv_c = _mix(hc, mod_c, p, ctx_tables)
        attn = lax.cond(_max_shift(g_q_head[l], g_k_head[l]) <= MAX_SAFE_SHIFT,
                        functools.partial(_attention, online=False),
                        functools.partial(_attention, online=True), q, k, v, kc, vc)
        h = ffn2(h, mod_l, merge_args=(attn, sgcv, p["g_out_att"], p["w_mix_out"]))
        if not last:
            attn_c = _attention(qc, kc, vc, online=True)
            hc = ffn2(flat(hc), mod_c1,
                      merge_args=(flat(attn_c), flat(sgcv_c), p["g_out_att"], p["w_mix_out"])).reshape(ctx.shape)
    return h
```

```python
import functools

import jax
import jax.numpy as jnp
from jax import lax
from jax.experimental import pallas as pl
from jax.experimental.pallas import tpu as pltpu

EPS = 1e-6
ADA_CHUNKS = 9
FFN_RESIDUAL_WEIGHT = 0.5
GRID_W = 64
ROPE_BASE = 10000.0
MLA_HEADS = 8
NOPE_DIM = 64
ROPE_DIM = 32
AXIS_DIM = ROPE_DIM // 2
QK_HEAD_DIM = NOPE_DIM + ROPE_DIM
V_HEAD_DIM = 64
Q_LORA = 384
KV_LORA = 256
ATT_WIDTH = MLA_HEADS * V_HEAD_DIM
SG_GROUPS = 4
SG_GROUP_DIM = 64
SG_WIDTH = SG_GROUPS * SG_GROUP_DIM
CHUNK = 128
CONV_WIDTH = 256
OFF_KV = Q_LORA
OFF_KPE = OFF_KV + KV_LORA
OFF_SG = OFF_KPE + ROPE_DIM
OFF_CONV = OFF_SG + 2 * SG_WIDTH
LOG2E = 1.4426950408889634
SHIFT_LANE = QK_HEAD_DIM
MAX_SAFE_SHIFT = 60.0

LANES = 128
BF16_ROWS = 16
VMEM_LIMIT_BYTES = 56 * 1024 * 1024

MIX_Q = 0
MIX_KPE = MIX_Q + Q_LORA
MIX_KV = MIX_KPE + LANES
MIX_SG = MIX_KV + KV_LORA
MIX_CV = MIX_SG + 2 * SG_WIDTH
MIX_COLS = MIX_CV + 3 * CONV_WIDTH

FF_CHUNK = 256
TOKEN_TILE = 512
FFN_TILE = 1024
FFN_SUBTILE = 512
Q_TILE = 512
KV_TILE = 256

BF16 = jnp.bfloat16
F32 = jnp.float32


def _dot(a, b):
    return jnp.dot(a, b, preferred_element_type=F32)


def _rms(x, g):
    return x * lax.rsqrt(jnp.mean(x * x, axis=-1, keepdims=True) + EPS) * g


def _silu(x):
    return x * jax.nn.sigmoid(x)


def _gelu_tanh(x):
    return 0.5 * x * (1.0 + jnp.tanh(0.7978845608028654 * (x + 0.044715 * (x * x * x))))


def _const_spec(shape):
    zeros = (0,) * len(shape)
    return pl.BlockSpec(shape, lambda *_: zeros, pipeline_mode=pl.Buffered(1))


def _ada_kernel(c_ref, w_ref, b_ref, o_ref):
    s = _silu(c_ref[...])
    w = w_ref[0]
    s_hi = s.astype(BF16)
    s_lo = (s - s_hi.astype(F32)).astype(BF16)
    w_hi = w.astype(BF16)
    w_lo = (w - w_hi.astype(F32)).astype(BF16)
    o_ref[0] = _dot(s_hi, w_hi) + _dot(s_lo, w_hi) + _dot(s_hi, w_lo) + b_ref[0]


def _ada(cvec, w_ada, b_ada):
    depth, d, n = w_ada.shape
    rows = cvec.shape[0]
    tn = d
    return pl.pallas_call(
        _ada_kernel,
        grid=(depth, n // tn),
        in_specs=[
            pl.BlockSpec((rows, d), lambda l, j: (0, 0)),
            pl.BlockSpec((1, d, tn), lambda l, j: (l, 0, j)),
            pl.BlockSpec((1, 1, tn), lambda l, j: (l, 0, j)),
        ],
        out_specs=pl.BlockSpec((1, rows, tn), lambda l, j: (l, 0, j)),
        out_shape=jax.ShapeDtypeStruct((depth, rows, n), F32),
        compiler_params=pltpu.CompilerParams(vmem_limit_bytes=VMEM_LIMIT_BYTES),
        name="ada_modulation",
    )(cvec, w_ada, b_ada.reshape(depth, 1, n))


def _ffn_kernel(*refs, merge, mod_base):
    if merge:
        (x_ref, mod_ref, attn_ref, sgcv_ref, gatt_ref, wmo_ref, g_ref, win_ref, wout_ref,
         o_ref, h_scr, acc_scr) = refs
    else:
        x_ref, mod_ref, g_ref, win_ref, wout_ref, o_ref, h_scr, acc_scr = refs
    d_ff = wout_ref.shape[0]
    tm = x_ref.shape[1]
    sub = min(FFN_SUBTILE, tm)
    subs = [slice(r, r + sub) for r in range(0, tm, sub)]
    shift = mod_ref[0, mod_base:mod_base + 1, :]
    scale = mod_ref[0, mod_base + 1:mod_base + 2, :]
    for rows in subs:
        x = x_ref[0, rows, :]
        if merge:
            att = _rms(attn_ref[0, rows, :], gatt_ref[...]).astype(BF16)
            y = _dot(att, wmo_ref[0:ATT_WIDTH, :]) + _dot(sgcv_ref[0, rows, :], wmo_ref[ATT_WIDTH:, :])
            x = x + mod_ref[0, 5:6, :] * y
        o_ref[0, rows, :] = x
        h_scr[rows, :] = (_rms(x, g_ref[...]) * (1.0 + scale) + shift).astype(BF16)

    def chunk(c, rows):
        a = _dot(h_scr[rows, :], win_ref[:, c:c + FF_CHUNK])
        b = _dot(h_scr[rows, :], win_ref[:, d_ff + c:d_ff + c + FF_CHUNK])
        return _dot((_silu(a) * b).astype(BF16), wout_ref[c:c + FF_CHUNK, :])

    for c in range(0, d_ff, FF_CHUNK):
        for rows in subs:
            if c == 0:
                acc_scr[rows, :] = chunk(c, rows)
            else:
                acc_scr[rows, :] += chunk(c, rows)
    gate = mod_ref[0, mod_base + 2:mod_base + 3, :]
    for rows in subs:
        o_ref[0, rows, :] = o_ref[0, rows, :] + (FFN_RESIDUAL_WEIGHT * gate) * acc_scr[rows, :]


def _ffn(x, mod, g, w_in, wout, *, layer, mod_base, merge_args=None):
    bsz, t, d = x.shape
    tm = min(FFN_TILE, t)
    assert wout.shape[1] % FF_CHUNK == 0
    slab = lambda w: pl.BlockSpec((None,) + w.shape[1:], lambda b, i: (layer, 0, 0), pipeline_mode=pl.Buffered(1))
    tile = lambda w: pl.BlockSpec((1, tm, w), lambda b, i: (b, i, 0))
    in_specs = [tile(d), pl.BlockSpec((1, ADA_CHUNKS, d), lambda b, i: (b, 0, 0))]
    args = [x, mod]
    if merge_args is not None:
        attn, sgcv, g_att, wmo = merge_args
        in_specs += [tile(ATT_WIDTH), tile(SG_WIDTH + CONV_WIDTH), _const_spec(g_att.shape), _const_spec(wmo.shape)]
        args += [attn, sgcv, g_att, wmo]
    in_specs += [_const_spec(g.shape), slab(w_in), slab(wout)]
    args += [g, w_in, wout]
    return pl.pallas_call(
        functools.partial(_ffn_kernel, merge=merge_args is not None, mod_base=mod_base),
        grid=(bsz, t // tm),
        in_specs=in_specs,
        out_specs=tile(d),
        out_shape=jax.ShapeDtypeStruct(x.shape, F32),
        scratch_shapes=[pltpu.VMEM((tm, d), BF16), pltpu.VMEM((tm, d), F32)],
        compiler_params=pltpu.CompilerParams(vmem_limit_bytes=VMEM_LIMIT_BYTES),
        name="ffn_merge" if merge_args is not None else "ffn",
    )(*args)


def _rope(xg, cos, sin_next, sin_prev):
    return xg * cos + pltpu.roll(xg, LANES - 8, 1) * sin_next + pltpu.roll(xg, 8, 1) * sin_prev


def _mix_kernel(x_ref, xp_ref, xn_ref, mod_ref, gmix_ref, w_ref, gql_ref, wqt_ref, gkvl_ref, wk_ref, wvt_ref,
                gqc_ref, gks_ref, gkpe_ref, shift_ref, vone_ref, cos_ref, sn_ref, sp_ref, cost_ref, sint_ref,
                gsgu_ref, wsp_ref, bsp_ref,
                wcv_ref, gosg_ref, gocv_ref, q_ref, k_ref, v_ref, sgcv_ref, h_scr, z_scr, *, tm, q_scale):
    i = pl.program_id(1)
    last = pl.num_programs(1) - 1
    halo = BF16_ROWS
    shift = mod_ref[0, 3:4, :]
    scale1 = 1.0 + mod_ref[0, 4:5, :]
    gmix = gmix_ref[...]

    def norm_mod(x):
        return (_rms(x, gmix) * scale1 + shift).astype(BF16)

    h_scr[0:halo] = norm_mod(xp_ref[0])
    h_scr[halo:halo + tm] = norm_mod(x_ref[0])
    h_scr[halo + tm:] = norm_mod(xn_ref[0])
    hm = h_scr[halo:halo + tm]

    cos = cos_ref[...]
    sn = sn_ref[...]
    sp = sp_ref[...]

    q_kpe = _dot(hm, w_ref[:, MIX_Q:MIX_KV])
    kvlat = _dot(hm, w_ref[:, MIX_KV:MIX_KV + KV_LORA])
    sgin = _dot(hm, w_ref[:, MIX_SG:MIX_SG + 2 * SG_WIDTH])
    cvin = _dot(h_scr[...], w_ref[:, MIX_CV:MIX_CV + 3 * CONV_WIDTH])

    qn = _rms(q_kpe[:, :Q_LORA], gql_ref[...]).astype(BF16)
    kvn = _rms(kvlat, gkvl_ref[...]).astype(BF16)
    qst = lax.dot_general(wqt_ref[...], qn, (((1,), (1,)), ((), ())), preferred_element_type=F32)
    kk = _dot(kvn, wk_ref[...])
    vt = lax.dot_general(wvt_ref[...], kvn, (((1,), (1,)), ((), ())), preferred_element_type=F32)
    cost = cost_ref[...]
    sint = sint_ref[...]
    gq = gqc_ref[...]
    lane = lax.broadcasted_iota(jnp.int32, (1, LANES), 1)
    half = AXIS_DIM // 2
    q_tail = jnp.broadcast_to(shift_ref[...], (LANES - QK_HEAD_DIM, tm)).astype(BF16)
    v_tail = jnp.broadcast_to(vone_ref[...], (LANES - V_HEAD_DIM, tm)).astype(BF16)
    for h in range(MLA_HEADS):
        xq = qst[h * QK_HEAD_DIM:(h + 1) * QK_HEAD_DIM]
        r = lax.rsqrt(jnp.sum(xq * xq, axis=0, keepdims=True) * (1.0 / QK_HEAD_DIM) + EPS) * q_scale
        xg = xq * gq
        pe = xg[NOPE_DIM:]
        swapped = jnp.concatenate([pe[half:2 * half], pe[0:half], pe[3 * half:4 * half], pe[2 * half:3 * half]], axis=0)
        y = jnp.concatenate([xg[0:NOPE_DIM], pe * cost + swapped * sint], axis=0)
        q_ref[0, h, 0:QK_HEAD_DIM, :] = (y * r).astype(BF16)
        q_ref[0, h, QK_HEAD_DIM:, :] = q_tail
        v_ref[0, h, 0:V_HEAD_DIM, :] = vt[h * V_HEAD_DIM:(h + 1) * V_HEAD_DIM].astype(BF16)
        v_ref[0, h, V_HEAD_DIM:, :] = v_tail

    kpe = q_kpe[:, Q_LORA:]
    ss_pe = jnp.sum(kpe * kpe, axis=-1, keepdims=True)
    pe = _rope(kpe * gkpe_ref[...], cos, sn, sp)
    k_one = jnp.where(lane == SHIFT_LANE, 1.0, 0.0)
    for h in range(MLA_HEADS):
        sl = slice(h * LANES, (h + 1) * LANES)
        xk = kk[:, sl]
        r = lax.rsqrt((jnp.sum(xk * xk, axis=-1, keepdims=True) + ss_pe) * (1.0 / QK_HEAD_DIM) + EPS)
        k_ref[0, h] = ((xk * gks_ref[:, sl] + pe) * r + k_one).astype(BF16)

    ge = _gelu_tanh(sgin)
    u = ge[:, :SG_WIDTH]
    low = lane < SG_GROUP_DIM
    vparts = []
    for c in range(SG_WIDTH // LANES):
        vv = ge[:, SG_WIDTH + c * LANES:SG_WIDTH + (c + 1) * LANES]
        v2 = vv * vv
        s_lo = jnp.sum(jnp.where(low, v2, 0.0), axis=-1, keepdims=True)
        s_hi = jnp.sum(jnp.where(low, 0.0, v2), axis=-1, keepdims=True)
        r = jnp.where(low, lax.rsqrt(s_lo * (1.0 / SG_GROUP_DIM) + EPS), lax.rsqrt(s_hi * (1.0 / SG_GROUP_DIM) + EPS))
        vparts.append(vv * r * gsgu_ref[:, c * LANES:(c + 1) * LANES])
    vn = jnp.concatenate(vparts, axis=-1)
    group = lax.broadcasted_iota(jnp.int32, (1, SG_WIDTH), 1) // SG_GROUP_DIM
    sg_rows = []
    for c in range(tm // CHUNK):
        vc = vn[c * CHUNK:(c + 1) * CHUNK]
        stacked = jnp.concatenate([jnp.where(group == g, vc, 0.0) for g in range(SG_GROUPS)], axis=0)
        vs = _dot(wsp_ref[...], stacked.astype(BF16)) + bsp_ref[...]
        sg_rows.append(u[c * CHUNK:(c + 1) * CHUNK] * vs)
    sg = jnp.concatenate(sg_rows, axis=0)
    sgcv_ref[0, :, 0:SG_WIDTH] = _rms(sg, gosg_ref[...]).astype(BF16)

    z =cvin[:, CONV_WIDTH:2 * CONV_WIDTH] * cvin[:, 2 * CONV_WIDTH:]
    z_scr[...] = z
    z_scr[0:halo] = jnp.where(i > 0, z[0:halo], 0.0)
    z_scr[halo + tm:] = jnp.where(i < last, z[halo + tm:], 0.0)
    y = (wcv_ref[0:1, :] * z_scr[halo - 1:halo - 1 + tm] + wcv_ref[1:2, :] * z_scr[halo:halo + tm]
         + wcv_ref[2:3, :] * z_scr[halo + 1:halo + 1 + tm])
    cv = cvin[halo:halo + tm, 0:CONV_WIDTH] * y
    sgcv_ref[0, :, SG_WIDTH:] = _rms(cv, gocv_ref[...]).astype(BF16)


def _mix(x, mod, p, tables):
    bsz, t, d = x.shape
    tm = min(TOKEN_TILE, t)
    halo = BF16_ROWS
    nblk = t // halo
    per = tm // halo
    cos, sn, sp, cost, sint = tables
    const_args = [p["g_mix"], p["w_mix"], p["g_q_lat"], p["w_qt"], p["g_kv_lat"], p["w_k"], p["w_vt"],
                  p["g_q_col"], p["g_k_slot"], p["g_k_pe"], p["q_shift"], p["v_one"]]
    tail_args = [p["g_sgu"], p["w_sp"], p["b_sp"], p["w_conv"], p["g_out_sg"], p["g_out_cv"]]
    tab_spec = pl.BlockSpec((tm, LANES), lambda b, i: (i, 0))
    tabt_spec = pl.BlockSpec((ROPE_DIM, tm), lambda b, i: (0, i))
    in_specs = ([pl.BlockSpec((1, tm, d), lambda b, i: (b, i, 0)),
                 pl.BlockSpec((1, halo, d), lambda b, i: (b, jnp.maximum(i * per - 1, 0), 0)),
                 pl.BlockSpec((1, halo, d), lambda b, i: (b, jnp.minimum((i + 1) * per, nblk - 1), 0)),
                 pl.BlockSpec((1, ADA_CHUNKS, d), lambda b, i: (b, 0, 0))]
                + [_const_spec(a.shape) for a in const_args]
                + [tab_spec, tab_spec, tab_spec, tabt_spec, tabt_spec]
                + [_const_spec(a.shape) for a in tail_args])
    head_spec = pl.BlockSpec((1, MLA_HEADS, tm, LANES), lambda b, i: (b, 0, i, 0))
    headt_spec = pl.BlockSpec((1, MLA_HEADS, LANES, tm), lambda b, i: (b, 0, 0, i))
    return pl.pallas_call(
        functools.partial(_mix_kernel, tm=tm, q_scale=LOG2E * QK_HEAD_DIM ** -0.5),
        grid=(bsz, t // tm),
        in_specs=in_specs,
        out_specs=[headt_spec, head_spec, headt_spec,
                   pl.BlockSpec((1, tm, SG_WIDTH + CONV_WIDTH), lambda b, i: (b, i, 0))],
        out_shape=[jax.ShapeDtypeStruct((bsz, MLA_HEADS, LANES, t), BF16),
                   jax.ShapeDtypeStruct((bsz, MLA_HEADS, t, LANES), BF16),
                   jax.ShapeDtypeStruct((bsz, MLA_HEADS, LANES, t), BF16),
                   jax.ShapeDtypeStruct((bsz, t, SG_WIDTH + CONV_WIDTH), BF16)],
        scratch_shapes=[pltpu.VMEM((tm + 2 * halo, d), BF16), pltpu.VMEM((tm + 2 * halo, CONV_WIDTH), F32)],
        compiler_params=pltpu.CompilerParams(vmem_limit_bytes=VMEM_LIMIT_BYTES),
        name="mixer_front",
    )(x, x, x, mod, *const_args, cos, sn, sp, cost, sint, *tail_args)


NEG_BIG = -1e30
VT_ROWS = LANES
SCORE_LEAD = 1


def _attn_kernel(*refs, tk, has_extra, online):
    if has_extra:
        q_ref, k_ref, vt_ref, k2_ref, vt2_ref, o_ref = refs
    else:
        q_ref, k_ref, vt_ref, o_ref = refs
    tq = q_ref.shape[3]
    qs = [q_ref[0, h] for h in range(2)]

    def consume(st, vth, state):
        if online:
            m, acc = state
            m_new = jnp.maximum(m, jnp.max(st, axis=0, keepdims=True))
            acc = jnp.exp2(m - m_new) * acc + _dot(vth, jnp.exp2(st - m_new).astype(BF16))
            return m_new, acc
        (acc,) = state
        return (acc + _dot(vth, jnp.exp2(st).astype(BF16)),)

    chunks = []
    for j in range(k_ref.shape[2] // tk):
        ks = slice(j * tk, (j + 1) * tk)
        chunks.append((lambda h, ks=ks: k_ref[0, h, ks, :], lambda h, ks=ks: vt_ref[0, h, 0:VT_ROWS, ks]))
    if has_extra:
        chunks.append((lambda h: k2_ref[0, h], lambda h: vt2_ref[0, h, 0:VT_ROWS, :]))

    acc0 = jnp.zeros((VT_ROWS, tq), F32)
    init = (jnp.full((1, tq), NEG_BIG, F32), acc0) if online else (acc0,)
    states = [init, init]
    n = len(chunks)
    score = lambda j: [_dot(chunks[j][0](h), qs[h]) for h in range(2)]
    pending = [score(j) for j in range(min(SCORE_LEAD, n))]
    for j in range(n):
        if j + SCORE_LEAD < n:
            pending.append(score(j + SCORE_LEAD))
        scores = pending.pop(0)
        states = [consume(scores[h], chunks[j][1](h), states[h]) for h in range(2)]
    outs = []
    for st in states:
        acc = st[-1]
        outs.append(acc[0:V_HEAD_DIM] / acc[V_HEAD_DIM:V_HEAD_DIM + 1])
    o_ref[0] = jnp.concatenate(outs, axis=0).T


def _attention(q, k, v, k2=None, v2=None, *, online):
    bsz, heads, _, t = q.shape
    t1 = k.shape[2]
    tq = min(Q_TILE, t)
    tk = min(KV_TILE, t1)
    has_extra = k2 is not None
    q_spec = pl.BlockSpec((1, 2, LANES, tq), lambda b, p, i: (b, p, 0, i))
    k_spec = lambda n: pl.BlockSpec((1, 2, n, LANES), lambda b, p, i: (b, p, 0, 0))
    vt_spec = lambda n: pl.BlockSpec((1, 2, LANES, n), lambda b, p, i: (b, p, 0, 0))
    in_specs = [q_spec, k_spec(t1), vt_spec(t1)]
    args = [q, k, v]
    if has_extra:
        in_specs += [k_spec(k2.shape[2]), vt_spec(k2.shape[2])]
        args += [k2, v2]
    return pl.pallas_call(
        functools.partial(_attn_kernel, tk=tk, has_extra=has_extra, online=online),
        grid=(bsz, heads // 2, t // tq),
        in_specs=in_specs,
        out_specs=pl.BlockSpec((1, tq, LANES), lambda b, p, i: (b, i, p)),
        out_shape=jax.ShapeDtypeStruct((bsz, t, ATT_WIDTH), F32),
        compiler_params=pltpu.CompilerParams(vmem_limit_bytes=VMEM_LIMIT_BYTES),
        name=("attention_online" if online else "attention_shifted") + ("_extra" if has_extra else ""),
    )(*args)


def _slot_pad(w, width):
    lead = w.shape[:-1]
    w = w.reshape(*lead, MLA_HEADS, width)
    w = jnp.pad(w, [(0, 0)] * len(lead) + [(0, 0), (0, LANES - width)])
    return w.reshape(*lead, MLA_HEADS * LANES)


CAST_BLOCK_BYTES = 4 * 1024 * 1024


def _cast_kernel(x_ref, o_ref):
    o_ref[...] = x_ref[...].astype(BF16)


def _cast_bf16(w):
    depth, r, c = w.shape
    rows = [tr for tr in range(BF16_ROWS, r + 1, BF16_ROWS) if r % tr == 0 and tr * c * 4 <= CAST_BLOCK_BYTES]
    tr = max(rows)
    spec = pl.BlockSpec((1, tr, c), lambda l, j: (l, j, 0))
    return pl.pallas_call(
        _cast_kernel, grid=(depth, r // tr), in_specs=[spec], out_specs=spec,
        out_shape=jax.ShapeDtypeStruct(w.shape, BF16),
        compiler_params=pltpu.CompilerParams(vmem_limit_bytes=VMEM_LIMIT_BYTES),
        name="cast_bf16",
    )(w)


def _max_shift(g_q, g_k):
    return (LOG2E * QK_HEAD_DIM ** 0.5) * jnp.max(jnp.abs(g_q)) * jnp.max(jnp.abs(g_k))


def _prep_layer(l, w):
    row = lambda a: a.reshape(1, -1).astype(F32)
    w_mix_in = w["w_mix_in"][l]
    d = w_mix_in.shape[0]
    kpe_cols = jnp.zeros((d, LANES), F32).at[:, NOPE_DIM:QK_HEAD_DIM].set(w_mix_in[:, OFF_KPE:OFF_SG])
    w_mix = jnp.concatenate([w_mix_in[:, :OFF_KV], kpe_cols, w_mix_in[:, OFF_KV:OFF_KPE], w_mix_in[:, OFF_SG:]],
                            axis=1).astype(BF16)
    w_kv = w["w_kv_up"][l].reshape(KV_LORA, MLA_HEADS, NOPE_DIM + V_HEAD_DIM)
    w_k = _slot_pad(w_kv[:, :, :NOPE_DIM].reshape(KV_LORA, -1), NOPE_DIM)
    w_v = w_kv[:, :, NOPE_DIM:].reshape(KV_LORA, ATT_WIDTH)
    g_k = w["g_k_head"][l]
    g_out = w["g_out"][l]
    w_sp = w["w_spatial"][l]
    return {
        "g_ffn1": row(w["g_ffn1"][l]), "g_ffn2": row(w["g_ffn2"][l]),
        "g_mix": row(w["g_mix"][l]), "w_mix": w_mix,
        "g_q_lat": row(w["g_q_lat"][l]), "w_qt": w["w_q_up"][l].T.astype(BF16),
        "g_kv_lat": row(w["g_kv_lat"][l]), "w_k": w_k.astype(BF16), "w_vt": w_v.T.astype(BF16),
        "g_q_col": w["g_q_head"][l].astype(F32).reshape(QK_HEAD_DIM, 1),
        "g_k_slot": row(_slot_pad(jnp.tile(g_k[:NOPE_DIM], MLA_HEADS), NOPE_DIM)),
        "g_k_pe": row(jnp.zeros((LANES,), F32).at[NOPE_DIM:QK_HEAD_DIM].set(g_k[NOPE_DIM:])),
        "q_shift": jnp.zeros((LANES - QK_HEAD_DIM, 1), F32).at[0, 0].set(-_max_shift(w["g_q_head"][l], g_k)),
        "v_one": jnp.zeros((LANES - V_HEAD_DIM, 1), F32).at[0, 0].set(1.0),
        "g_sgu": row(w["g_sgu"][l]),
        "w_sp": jnp.concatenate([w_sp[g] for g in range(SG_GROUPS)], axis=1).astype(BF16),
        "b_sp": jnp.repeat(w["b_spatial"][l].T, SG_GROUP_DIM, axis=1).astype(F32),
        "w_conv": w["w_conv"][l].astype(F32),
        "g_out_att": row(g_out[:ATT_WIDTH]),
        "g_out_sg": row(g_out[ATT_WIDTH:ATT_WIDTH + SG_WIDTH]),
        "g_out_cv": row(g_out[ATT_WIDTH + SG_WIDTH:]),
        "w_mix_out": w["w_mix_out"][l].astype(BF16),
    }


def _rope_tables(n):
    rows = n // GRID_W
    row = jnp.repeat(jnp.arange(rows), GRID_W).astype(F32)
    col = jnp.tile(jnp.arange(GRID_W), rows).astype(F32)
    inv = 1.0 / (ROPE_BASE ** (jnp.arange(0, AXIS_DIM, 2, dtype=F32) / AXIS_DIM))
    ang_r = row[:, None] * inv
    ang_c = col[:, None] * inv
    ang = jnp.concatenate([ang_r, ang_r, ang_c, ang_c], axis=-1)
    cos, sin = jnp.cos(ang), jnp.sin(ang)
    first_half = (jnp.arange(ROPE_DIM) % AXIS_DIM) < AXIS_DIM // 2
    pad = lambda a, fill: jnp.pad(a, ((0, 0), (NOPE_DIM, LANES - QK_HEAD_DIM)), constant_values=fill)
    return (pad(cos, 1.0), pad(jnp.where(first_half, -sin, 0.0), 0.0), pad(jnp.where(first_half, 0.0, sin), 0.0),
            cos.T, jnp.where(first_half, -sin, sin).T)


def _identity_tables(n):
    return (jnp.ones((n, LANES), F32), jnp.zeros((n, LANES), F32), jnp.zeros((n, LANES), F32),
            jnp.ones((ROPE_DIM, n), F32), jnp.zeros((ROPE_DIM, n), F32))


def kernel(x, c, ctx, c_ctx, w_ada, b_ada, g_ffn1, w_ffn1_in, w_ffn1_out, g_mix, w_mix_in, g_q_lat, w_q_up, g_kv_lat, w_kv_up, g_q_head, g_k_head, g_sgu, w_spatial, b_spatial, w_conv, g_out, w_mix_out, g_ffn2, w_ffn2_in, w_ffn2_out):
    weights = dict(g_ffn1=g_ffn1, w_ffn1_in=w_ffn1_in, w_ffn1_out=w_ffn1_out, g_mix=g_mix, w_mix_in=w_mix_in,
                   g_q_lat=g_q_lat, w_q_up=w_q_up, g_kv_lat=g_kv_lat, w_kv_up=w_kv_up, g_q_head=g_q_head,
                   g_k_head=g_k_head, g_sgu=g_sgu, w_spatial=w_spatial, b_spatial=b_spatial, w_conv=w_conv,
                   g_out=g_out, w_mix_out=w_mix_out, g_ffn2=g_ffn2, w_ffn2_in=w_ffn2_in, w_ffn2_out=w_ffn2_out)
    bsz, t, d = x.shape
    depth = w_ada.shape[0]
    rows = 8 * ((bsz + 1 + 7) // 8)
    cvec = jnp.zeros((rows, d), F32).at[:bsz].set(c).at[bsz].set(c_ctx)
    mod = _ada(cvec, w_ada, b_ada)
    lat_tables = _rope_tables(t)
    ctx_tables = _identity_tables(ctx.shape[1])

    h, hc = x, ctx
    ffn1_in, ffn1_out = _cast_bf16(w_ffn1_in), _cast_bf16(w_ffn1_out)
    ffn2_in, ffn2_out = _cast_bf16(w_ffn2_in), _cast_bf16(w_ffn2_out)
    for l in range(depth):
        last = l == depth - 1
        p = _prep_layer(l, weights)
        mod_l = mod[l, :bsz].reshape(bsz, ADA_CHUNKS, d)
        mod_c1 = mod[l, bsz].reshape(1, ADA_CHUNKS, d)
        mod_c = jnp.broadcast_to(mod_c1, (bsz, ADA_CHUNKS, d))
        flat = lambda a: a.reshape(1, -1, a.shape[-1])

        ffn1 = functools.partial(_ffn, g=p["g_ffn1"], w_in=ffn1_in, wout=ffn1_out, layer=l, mod_base=0)
        ffn2 = functools.partial(_ffn, g=p["g_ffn2"], w_in=ffn2_in, wout=ffn2_out, layer=l, mod_base=6)

        h = ffn1(h, mod_l)
        hc = ffn1(flat(hc), mod_c1).reshape(ctx.shape)

        q, k, v, sgcv = _mix(h, mod_l, p, lat_tables)
        qc, kc, vc, sgcv_c = _mix(hc, mod_c, p, ctx_tables)
        attn = lax.cond(_max_shift(g_q_head[l], g_k_head[l]) <= MAX_SAFE_SHIFT,
                        functools.partial(_attention, online=False),
                        functools.partial(_attention, online=True), q, k, v, kc, vc)
        h = ffn2(h, mod_l, merge_args=(attn, sgcv, p["g_out_att"], p["w_mix_out"]))
        if not last:
            attn_c = _attention(qc, kc, vc, online=True)
            hc = ffn2(flat(hc), mod_c1,
                      merge_args=(flat(attn_c), flat(sgcv_c), p["g_out_att"], p["w_mix_out"])).reshape(ctx.shape)
    return h
```

```python
import functools

import jax
import jax.numpy as jnp
from jax import lax
from jax.experimental import pallas as pl
from jax.experimental.pallas import tpu as pltpu

EPS = 1e-6
ADA_CHUNKS = 9
FFN_RESIDUAL_WEIGHT = 0.5
GRID_W = 64
ROPE_BASE = 10000.0
MLA_HEADS = 8
NOPE_DIM = 64
ROPE_DIM = 32
AXIS_DIM = ROPE_DIM // 2
QK_HEAD_DIM = NOPE_DIM + ROPE_DIM
V_HEAD_DIM = 64
Q_LORA = 384
KV_LORA = 256
ATT_WIDTH = MLA_HEADS * V_HEAD_DIM
SG_GROUPS = 4
SG_GROUP_DIM = 64
SG_WIDTH = SG_GROUPS * SG_GROUP_DIM
CHUNK = 128
CONV_WIDTH = 256
OFF_KV = Q_LORA
OFF_KPE = OFF_KV + KV_LORA
OFF_SG = OFF_KPE + ROPE_DIM
LOG2E = 1.4426950408889634
SHIFT_LANE = QK_HEAD_DIM
MAX_SAFE_SHIFT = 60.0

LANES = 128
BF16_ROWS = 16
VMEM_CAPACITY_BYTES = 64 * 1024 * 1024
VMEM_LIMIT_BYTES = VMEM_CAPACITY_BYTES * 7 // 8

MIX_Q = 0
MIX_KPE = MIX_Q + Q_LORA
MIX_KV = MIX_KPE + LANES
MIX_SG = MIX_KV + KV_LORA
MIX_CV = MIX_SG + 2 * SG_WIDTH

FF_CHUNK = 256
TOKEN_TILE = 1024
FFN_TILE = 1024
FFN_SUBTILE = 512
Q_TILE = 512
KV_TILE = 256

BF16 = jnp.bfloat16
F32 = jnp.float32


def _dot(a, b):
    return jnp.dot(a, b, preferred_element_type=F32)


def _rms(x, g):
    return x * lax.rsqrt(jnp.mean(x * x, axis=-1, keepdims=True) + EPS) * g


def _silu(x):
    return x * jax.nn.sigmoid(x)


def _gelu_tanh(x):
    return 0.5 * x * (1.0 + jnp.tanh(0.7978845608028654 * (x + 0.044715 * (x * x * x))))


def _const_spec(shape):
    zeros = (0,) * len(shape)
    return pl.BlockSpec(shape, lambda *_: zeros, pipeline_mode=pl.Buffered(1))


def _ada_kernel(c_ref, w_ref, b_ref, o_ref):
    s = _silu(c_ref[...])
    w = w_ref[0]
    s_hi = s.astype(BF16)
    s_lo = (s - s_hi.astype(F32)).astype(BF16)
    w_hi = w.astype(BF16)
    w_lo = (w - w_hi.astype(F32)).astype(BF16)
    o_ref[0] = _dot(s_hi, w_hi) + _dot(s_lo, w_hi) + _dot(s_hi, w_lo) + b_ref[0]


def _ada(cvec, w_ada, b_ada):
    depth, d, n = w_ada.shape
    rows = cvec.shape[0]
    tn = d
    return pl.pallas_call(
        _ada_kernel,
        grid=(depth, n // tn),
        in_specs=[
            pl.BlockSpec((rows, d), lambda l, j: (0, 0)),
            pl.BlockSpec((1, d, tn), lambda l, j: (l, 0, j)),
            pl.BlockSpec((1, 1, tn), lambda l, j: (l, 0, j)),
        ],
        out_specs=pl.BlockSpec((1, rows, tn), lambda l, j: (l, 0, j)),
        out_shape=jax.ShapeDtypeStruct((depth, rows, n), F32),
        compiler_params=pltpu.CompilerParams(vmem_limit_bytes=VMEM_LIMIT_BYTES),
        name="ada_modulation",
    )(cvec, w_ada, b_ada.reshape(depth, 1, n))


def _ffn_kernel(*refs, merge, mod_base):
    if merge:
        (x_ref, mod_ref, attn_ref, sgcv_ref, gatt_ref, wmo_ref, g_ref, win_ref, wout_ref,
         o_ref, h_scr, acc_scr) = refs
    else:
        x_ref, mod_ref, g_ref, win_ref, wout_ref, o_ref, h_scr, acc_scr = refs
    d_ff = wout_ref.shape[0]
    tm = x_ref.shape[1]
    sub = min(FFN_SUBTILE, tm)
    subs = [slice(r, r + sub) for r in range(0, tm, sub)]
    shift = mod_ref[0, mod_base:mod_base + 1, :]
    scale = mod_ref[0, mod_base + 1:mod_base + 2, :]
    for rows in subs:
        x = x_ref[0, rows, :]
        if merge:
            att = _rms(attn_ref[0, rows, :], gatt_ref[...]).astype(BF16)
            y = _dot(att, wmo_ref[0:ATT_WIDTH, :]) + _dot(sgcv_ref[0, rows, :], wmo_ref[ATT_WIDTH:, :])
            x = x + mod_ref[0, 5:6, :] * y
        o_ref[0, rows, :] = x
        h_scr[rows, :] = (_rms(x, g_ref[...]) * (1.0 + scale) + shift).astype(BF16)

    def chunk(c, rows):
        a = _dot(h_scr[rows, :], win_ref[:, c:c + FF_CHUNK])
        b = _dot(h_scr[rows, :], win_ref[:, d_ff + c:d_ff + c + FF_CHUNK])
        return _dot((_silu(a) * b).astype(BF16), wout_ref[c:c + FF_CHUNK, :])

    for c in range(0, d_ff, FF_CHUNK):
        for rows in subs:
            if c == 0:
                acc_scr[rows, :] = chunk(c, rows)
            else:
                acc_scr[rows, :] += chunk(c, rows)
    gate = mod_ref[0, mod_base + 2:mod_base + 3, :]
    for rows in subs:
        o_ref[0, rows, :] = o_ref[0, rows, :] + (FFN_RESIDUAL_WEIGHT * gate) * acc_scr[rows, :]


def _ffn(x, mod, g, w_in, wout, *, layer, mod_base, merge_args=None):
    bsz, t, d = x.shape
    tm = min(FFN_TILE, t)
    assert wout.shape[1] % FF_CHUNK == 0
    slab = lambda w: pl.BlockSpec((None,) + w.shape[1:], lambda b, i: (layer, 0, 0), pipeline_mode=pl.Buffered(1))
    tile = lambda w: pl.BlockSpec((1, tm, w), lambda b, i: (b, i, 0))
    in_specs = [tile(d), pl.BlockSpec((1, ADA_CHUNKS, d), lambda b, i: (b, 0, 0))]
    args = [x, mod]
    if merge_args is not None:
        attn, sgcv, g_att, wmo = merge_args
        in_specs += [tile(ATT_WIDTH), tile(SG_WIDTH + CONV_WIDTH), _const_spec(g_att.shape), _const_spec(wmo.shape)]
        args += [attn, sgcv, g_att, wmo]
    in_specs += [_const_spec(g.shape), slab(w_in), slab(wout)]
    args += [g, w_in, wout]
    return pl.pallas_call(
        functools.partial(_ffn_kernel, merge=merge_args is not None, mod_base=mod_base),
        grid=(bsz, t // tm),
        in_specs=in_specs,
        out_specs=tile(d),
        out_shape=jax.ShapeDtypeStruct(x.shape, F32),
        scratch_shapes=[pltpu.VMEM((tm, d), BF16), pltpu.VMEM((tm, d), F32)],
        compiler_params=pltpu.CompilerParams(vmem_limit_bytes=VMEM_LIMIT_BYTES),
        name="ffn_merge" if merge_args is not None else "ffn",
    )(*args)


def _rope(xg, cos, sin_next, sin_prev):
    return xg * cos + pltpu.roll(xg, LANES - 8, 1) * sin_next + pltpu.roll(xg, 8, 1) * sin_prev


def _mix_kernel(x_ref, xp_ref, xn_ref, mod_ref, gmix_ref, w_ref, gql_ref, wqt_ref, gkvl_ref, wk_ref, wvt_ref,
                gqc_ref, gks_ref, gkpe_ref, shift_ref, vone_ref, cos_ref, sn_ref, sp_ref, cost_ref, sint_ref,
                gsgu_ref, wsp_ref, bsp_ref,
                wcv_ref, gosg_ref, gocv_ref, q_ref, k_ref, v_ref, sgcv_ref, h_scr, z_scr, *, tm, q_scale):
    i = pl.program_id(1)
    last = pl.num_programs(1) - 1
    halo = BF16_ROWS
    shift = mod_ref[0, 3:4, :]
    scale1 = 1.0 + mod_ref[0, 4:5, :]
    gmix = gmix_ref[...]

    def norm_mod(x):
        return (_rms(x, gmix) * scale1 + shift).astype(BF16)

    h_scr[0:halo] = norm_mod(xp_ref[0])
    h_scr[halo:halo + tm] = norm_mod(x_ref[0])
    h_scr[halo + tm:] = norm_mod(xn_ref[0])
    hm = h_scr[halo:halo + tm]

    cos = cos_ref[...]
    sn = sn_ref[...]
    sp = sp_ref[...]

    q_kpe = _dot(hm, w_ref[:, MIX_Q:MIX_KV])
    kvlat = _dot(hm, w_ref[:, MIX_KV:MIX_KV + KV_LORA])
    sgin = _dot(hm, w_ref[:, MIX_SG:MIX_SG + 2 * SG_WIDTH])
    cvin = _dot(h_scr[...], w_ref[:, MIX_CV:MIX_CV + 3 * CONV_WIDTH])

    qn = _rms(q_kpe[:, :Q_LORA], gql_ref[...]).astype(BF16)
    kvn = _rms(kvlat, gkvl_ref[...]).astype(BF16)
    qst = lax.dot_general(wqt_ref[...], qn, (((1,), (1,)), ((), ())), preferred_element_type=F32)
    kk = _dot(kvn, wk_ref[...])
    vt = lax.dot_general(wvt_ref[...], kvn, (((1,), (1,)), ((), ())), preferred_element_type=F32)
    cost = cost_ref[...]
    sint = sint_ref[...]
    gq = gqc_ref[...]
    lane = lax.broadcasted_iota(jnp.int32, (1, LANES), 1)
    half = AXIS_DIM // 2
    q_tail = jnp.broadcast_to(shift_ref[...], (LANES - QK_HEAD_DIM, tm)).astype(BF16)
    v_tail = jnp.broadcast_to(vone_ref[...], (LANES - V_HEAD_DIM, tm)).astype(BF16)
    for h in range(MLA_HEADS):
        xq = qst[h * QK_HEAD_DIM:(h + 1) * QK_HEAD_DIM]
        r = lax.rsqrt(jnp.sum(xq * xq, axis=0, keepdims=True) * (1.0 / QK_HEAD_DIM) + EPS) * q_scale
        xg = xq * gq
        pe = xg[NOPE_DIM:]
        swapped = jnp.concatenate([pe[half:2 * half], pe[0:half], pe[3 * half:4 * half], pe[2 * half:3 * half]], axis=0)
        y = jnp.concatenate([xg[0:NOPE_DIM], pe * cost + swapped * sint], axis=0)
        q_ref[0, h, 0:QK_HEAD_DIM, :] = (y * r).astype(BF16)
        q_ref[0, h, QK_HEAD_DIM:, :] = q_tail
        v_ref[0, h, 0:V_HEAD_DIM, :] = vt[h * V_HEAD_DIM:(h + 1) * V_HEAD_DIM].astype(BF16)
        v_ref[0, h, V_HEAD_DIM:, :] = v_tail

    kpe = q_kpe[:, Q_LORA:]
    ss_pe = jnp.sum(kpe * kpe, axis=-1, keepdims=True)
    pe = _rope(kpe * gkpe_ref[...], cos, sn, sp)
    k_one = jnp.where(lane == SHIFT_LANE, 1.0, 0.0)
    for h in range(MLA_HEADS):
        sl = slice(h * LANES, (h + 1) * LANES)
        xk = kk[:, sl]
        r = lax.rsqrt((jnp.sum(xk * xk, axis=-1, keepdims=True) + ss_pe) * (1.0 / QK_HEAD_DIM) + EPS)
        k_ref[0, h] = ((xk * gks_ref[:, sl] + pe) * r + k_one).astype(BF16)

    ge = _gelu_tanh(sgin)
    u = ge[:, :SG_WIDTH]
    low = lane < SG_GROUP_DIM
    vparts = []
    for c in range(SG_WIDTH // LANES):
        vv = ge[:, SG_WIDTH + c * LANES:SG_WIDTH + (c + 1) * LANES]
        v2 = vv * vv
        s_lo = jnp.sum(jnp.where(low, v2, 0.0), axis=-1, keepdims=True)
        s_hi = jnp.sum(jnp.where(low, 0.0, v2), axis=-1, keepdims=True)
        r = jnp.where(low, lax.rsqrt(s_lo * (1.0 / SG_GROUP_DIM) + EPS), lax.rsqrt(s_hi * (1.0 / SG_GROUP_DIM) + EPS))
        vparts.append(vv * r * gsgu_ref[:, c * LANES:(c + 1) * LANES])
    vn = jnp.concatenate(vparts, axis=-1)
    group = lax.broadcasted_iota(jnp.int32, (1, SG_WIDTH), 1) // SG_GROUP_DIM
    sg_rows = []
    for c in range(tm // CHUNK):
        vc = vn[c * CHUNK:(c + 1) * CHUNK]
        stacked = jnp.concatenate([jnp.where(group == g, vc, 0.0) for g in range(SG_GROUPS)], axis=0)
        vs = _dot(wsp_ref[...], stacked.astype(BF16)) + bsp_ref[...]
        sg_rows.append(u[c * CHUNK:(c + 1) * CHUNK] * vs)
    sg = jnp.concatenate(sg_rows, axis=0)
    sgcv_ref[0, :, 0:SG_WIDTH] = _rms(sg, gosg_ref[...]).astype(BF16)

    z =cvin[:, CONV_WIDTH:2 * CONV_WIDTH] * cvin[:, 2 * CONV_WIDTH:]
    z_scr[...] = z
    z_scr[0:halo] = jnp.where(i > 0, z[0:halo], 0.0)
    z_scr[halo + tm:] = jnp.where(i < last, z[halo + tm:], 0.0)
    y = (wcv_ref[0:1, :] * z_scr[halo - 1:halo - 1 + tm] + wcv_ref[1:2, :] * z_scr[halo:halo + tm]
         + wcv_ref[2:3, :] * z_scr[halo + 1:halo + 1 + tm])
    cv = cvin[halo:halo + tm, 0:CONV_WIDTH] * y
    sgcv_ref[0, :, SG_WIDTH:] = _rms(cv, gocv_ref[...]).astype(BF16)


def _mix(x, mod, p, tables):
    bsz, t, d = x.shape
    tm = min(TOKEN_TILE, t)
    halo = BF16_ROWS
    nblk = t // halo
    per = tm // halo
    cos, sn, sp, cost, sint = tables
    const_args = [p["g_mix"], p["w_mix"], p["g_q_lat"], p["w_qt"], p["g_kv_lat"], p["w_k"], p["w_vt"],
                  p["g_q_col"], p["g_k_slot"], p["g_k_pe"], p["q_shift"], p["v_one"]]
    tail_args = [p["g_sgu"], p["w_sp"], p["b_sp"], p["w_conv"], p["g_out_sg"], p["g_out_cv"]]
    tab_spec = pl.BlockSpec((tm, LANES), lambda b, i: (i, 0))
    tabt_spec = pl.BlockSpec((ROPE_DIM, tm), lambda b, i: (0, i))
    in_specs = ([pl.BlockSpec((1, tm, d), lambda b, i: (b, i, 0)),
                 pl.BlockSpec((1, halo, d), lambda b, i: (b, jnp.maximum(i * per - 1, 0), 0)),
                 pl.BlockSpec((1, halo, d), lambda b, i: (b, jnp.minimum((i + 1) * per, nblk - 1), 0)),
                 pl.BlockSpec((1, ADA_CHUNKS, d), lambda b, i: (b, 0, 0))]
                + [_const_spec(a.shape) for a in const_args]
                + [tab_spec, tab_spec, tab_spec, tabt_spec, tabt_spec]
                + [_const_spec(a.shape) for a in tail_args])
    head_spec = pl.BlockSpec((1, MLA_HEADS, tm, LANES), lambda b, i: (b, 0, i, 0))
    headt_spec = pl.BlockSpec((1, MLA_HEADS, LANES, tm), lambda b, i: (b, 0, 0, i))
    return pl.pallas_call(
        functools.partial(_mix_kernel, tm=tm, q_scale=LOG2E * QK_HEAD_DIM ** -0.5),
        grid=(bsz, t // tm),
        in_specs=in_specs,
        out_specs=[headt_spec, head_spec, headt_spec,
                   pl.BlockSpec((1, tm, SG_WIDTH + CONV_WIDTH), lambda b, i: (b, i, 0))],
        out_shape=[jax.ShapeDtypeStruct((bsz, MLA_HEADS, LANES, t), BF16),
                   jax.ShapeDtypeStruct((bsz, MLA_HEADS, t, LANES), BF16),
                   jax.ShapeDtypeStruct((bsz, MLA_HEADS, LANES, t), BF16),
                   jax.ShapeDtypeStruct((bsz, t, SG_WIDTH + CONV_WIDTH), BF16)],
        scratch_shapes=[pltpu.VMEM((tm + 2 * halo, d), BF16), pltpu.VMEM((tm + 2 * halo, CONV_WIDTH), F32)],
        compiler_params=pltpu.CompilerParams(vmem_limit_bytes=VMEM_LIMIT_BYTES),
        name="mixer_front",
    )(x, x, x, mod, *const_args, cos, sn, sp, cost, sint, *tail_args)


NEG_BIG = -1e30
VT_ROWS = LANES
SCORE_LEAD = 1


def _attn_kernel(*refs, tk, has_extra, online):
    if has_extra:
        q_ref, k_ref, vt_ref, k2_ref, vt2_ref, o_ref = refs
    else:
        q_ref, k_ref, vt_ref, o_ref = refs
    tq = q_ref.shape[3]
    qs = [q_ref[0, h] for h in range(2)]

    def consume(st, vth, state):
        if online:
            m, acc = state
            m_new = jnp.maximum(m, jnp.max(st, axis=0, keepdims=True))
            acc = jnp.exp2(m - m_new) * acc + _dot(vth, jnp.exp2(st - m_new).astype(BF16))
            return m_new, acc
        (acc,) = state
        return (acc + _dot(vth, jnp.exp2(st).astype(BF16)),)

    chunks = []
    for j in range(k_ref.shape[2] // tk):
        ks = slice(j * tk, (j + 1) * tk)
        chunks.append((lambda h, ks=ks: k_ref[0, h, ks, :], lambda h, ks=ks: vt_ref[0, h, 0:VT_ROWS, ks]))
    if has_extra:
        chunks.append((lambda h: k2_ref[0, h], lambda h: vt2_ref[0, h, 0:VT_ROWS, :]))

    acc0 = jnp.zeros((VT_ROWS, tq), F32)
    init = (jnp.full((1, tq), NEG_BIG, F32), acc0) if online else (acc0,)
    states = [init, init]
    n = len(chunks)
    score = lambda j: [_dot(chunks[j][0](h), qs[h]) for h in range(2)]
    pending = [score(j) for j in range(min(SCORE_LEAD, n))]
    for j in range(n):
        if j + SCORE_LEAD < n:
            pending.append(score(j + SCORE_LEAD))
        scores = pending.pop(0)
        states = [consume(scores[h], chunks[j][1](h), states[h]) for h in range(2)]
    outs = []
    for st in states:
        acc = st[-1]
        outs.append(acc[0:V_HEAD_DIM] / acc[V_HEAD_DIM:V_HEAD_DIM + 1])
    o_ref[0] = jnp.concatenate(outs, axis=0).T


def _attention(q, k, v, k2=None, v2=None, *, online):
    bsz, heads, _, t = q.shape
    t1 = k.shape[2]
    tq = min(Q_TILE, t)
    tk = min(KV_TILE, t1)
    has_extra = k2 is not None
    q_spec = pl.BlockSpec((1, 2, LANES, tq), lambda b, p, i: (b, p, 0, i))
    k_spec = lambda n: pl.BlockSpec((1, 2, n, LANES), lambda b, p, i: (b, p, 0, 0))
    vt_spec = lambda n: pl.BlockSpec((1, 2, LANES, n), lambda b, p, i: (b, p, 0, 0))
    in_specs = [q_spec, k_spec(t1), vt_spec(t1)]
    args = [q, k, v]
    if has_extra:
        in_specs += [k_spec(k2.shape[2]), vt_spec(k2.shape[2])]
        args += [k2, v2]
    return pl.pallas_call(
        functools.partial(_attn_kernel, tk=tk, has_extra=has_extra, online=online),
        grid=(bsz, heads // 2, t // tq),
        in_specs=in_specs,
        out_specs=pl.BlockSpec((1, tq, LANES), lambda b, p, i: (b, i, p)),
        out_shape=jax.ShapeDtypeStruct((bsz, t, ATT_WIDTH), F32),
        compiler_params=pltpu.CompilerParams(vmem_limit_bytes=VMEM_LIMIT_BYTES),
        name=("attention_online" if online else "attention_shifted") + ("_extra" if has_extra else ""),
    )(*args)


def _slot_pad(w, width):
    lead = w.shape[:-1]
    w = w.reshape(*lead, MLA_HEADS, width)
    w = jnp.pad(w, [(0, 0)] * len(lead) + [(0, 0), (0, LANES - width)])
    return w.reshape(*lead, MLA_HEADS * LANES)


CAST_BLOCK_BYTES = 4 * 1024 * 1024


def _cast_kernel(x_ref, o_ref):
    o_ref[...] = x_ref[...].astype(BF16)


def _cast_bf16(w):
    depth, r, c = w.shape
    rows = [tr for tr in range(BF16_ROWS, r + 1, BF16_ROWS) if r % tr == 0 and tr * c * 4 <= CAST_BLOCK_BYTES]
    tr = max(rows)
    spec = pl.BlockSpec((1, tr, c), lambda l, j: (l, j, 0))
    return pl.pallas_call(
        _cast_kernel, grid=(depth, r // tr), in_specs=[spec], out_specs=spec,
        out_shape=jax.ShapeDtypeStruct(w.shape, BF16),
        compiler_params=pltpu.CompilerParams(vmem_limit_bytes=VMEM_LIMIT_BYTES),
        name="cast_bf16",
    )(w)


def _max_shift(g_q, g_k):
    return (LOG2E * QK_HEAD_DIM ** 0.5) * jnp.max(jnp.abs(g_q)) * jnp.max(jnp.abs(g_k))


def _prep_layer(l, w):
    row = lambda a: a.reshape(1, -1).astype(F32)
    w_mix_in = w["w_mix_in"][l]
    d = w_mix_in.shape[0]
    kpe_cols = jnp.zeros((d, LANES), F32).at[:, NOPE_DIM:QK_HEAD_DIM].set(w_mix_in[:, OFF_KPE:OFF_SG])
    w_mix = jnp.concatenate([w_mix_in[:, :OFF_KV], kpe_cols, w_mix_in[:, OFF_KV:OFF_KPE], w_mix_in[:, OFF_SG:]],
                            axis=1).astype(BF16)
    w_kv = w["w_kv_up"][l].reshape(KV_LORA, MLA_HEADS, NOPE_DIM + V_HEAD_DIM)
    w_k = _slot_pad(w_kv[:, :, :NOPE_DIM].reshape(KV_LORA, -1), NOPE_DIM)
    w_v = w_kv[:, :, NOPE_DIM:].reshape(KV_LORA, ATT_WIDTH)
    g_k = w["g_k_head"][l]
    g_out = w["g_out"][l]
    w_sp = w["w_spatial"][l]
    return {
        "g_ffn1": row(w["g_ffn1"][l]), "g_ffn2": row(w["g_ffn2"][l]),
        "g_mix": row(w["g_mix"][l]), "w_mix": w_mix,
        "g_q_lat": row(w["g_q_lat"][l]), "w_qt": w["w_q_up"][l].T.astype(BF16),
        "g_kv_lat": row(w["g_kv_lat"][l]), "w_k": w_k.astype(BF16), "w_vt": w_v.T.astype(BF16),
        "g_q_col": w["g_q_head"][l].astype(F32).reshape(QK_HEAD_DIM, 1),
        "g_k_slot": row(_slot_pad(jnp.tile(g_k[:NOPE_DIM], MLA_HEADS), NOPE_DIM)),
        "g_k_pe": row(jnp.zeros((LANES,), F32).at[NOPE_DIM:QK_HEAD_DIM].set(g_k[NOPE_DIM:])),
        "q_shift": jnp.zeros((LANES - QK_HEAD_DIM, 1), F32).at[0, 0].set(-_max_shift(w["g_q_head"][l], g_k)),
        "v_one": jnp.zeros((LANES - V_HEAD_DIM, 1), F32).at[0, 0].set(1.0),
        "g_sgu": row(w["g_sgu"][l]),
        "w_sp": jnp.concatenate([w_sp[g] for g in range(SG_GROUPS)], axis=1).astype(BF16),
        "b_sp": jnp.repeat(w["b_spatial"][l].T, SG_GROUP_DIM, axis=1).astype(F32),
        "w_conv": w["w_conv"][l].astype(F32),
        "g_out_att": row(g_out[:ATT_WIDTH]),
        "g_out_sg": row(g_out[ATT_WIDTH:ATT_WIDTH + SG_WIDTH]),
        "g_out_cv": row(g_out[ATT_WIDTH + SG_WIDTH:]),
        "w_mix_out": w["w_mix_out"][l].astype(BF16),
    }


def _rope_tables(n):
    rows = n // GRID_W
    row = jnp.repeat(jnp.arange(rows), GRID_W).astype(F32)
    col = jnp.tile(jnp.arange(GRID_W), rows).astype(F32)
    inv = 1.0 / (ROPE_BASE ** (jnp.arange(0, AXIS_DIM, 2, dtype=F32) / AXIS_DIM))
    ang_r = row[:, None] * inv
    ang_c = col[:, None] * inv
    ang = jnp.concatenate([ang_r, ang_r, ang_c, ang_c], axis=-1)
    cos, sin = jnp.cos(ang), jnp.sin(ang)
    first_half = (jnp.arange(ROPE_DIM) % AXIS_DIM) < AXIS_DIM // 2
    pad = lambda a, fill: jnp.pad(a, ((0, 0), (NOPE_DIM, LANES - QK_HEAD_DIM)), constant_values=fill)
    return (pad(cos, 1.0), pad(jnp.where(first_half, -sin, 0.0), 0.0), pad(jnp.where(first_half, 0.0, sin), 0.0),
            cos.T, jnp.where(first_half, -sin, sin).T)


def _identity_tables(n):
    return (jnp.ones((n, LANES), F32), jnp.zeros((n, LANES), F32), jnp.zeros((n, LANES), F32),
            jnp.ones((ROPE_DIM, n), F32), jnp.zeros((ROPE_DIM, n), F32))


def kernel(x, c, ctx, c_ctx, w_ada, b_ada, g_ffn1, w_ffn1_in, w_ffn1_out, g_mix, w_mix_in, g_q_lat, w_q_up, g_kv_lat, w_kv_up, g_q_head, g_k_head, g_sgu, w_spatial, b_spatial, w_conv, g_out, w_mix_out, g_ffn2, w_ffn2_in, w_ffn2_out):
    weights = dict(g_ffn1=g_ffn1, w_ffn1_in=w_ffn1_in, w_ffn1_out=w_ffn1_out, g_mix=g_mix, w_mix_in=w_mix_in,
                   g_q_lat=g_q_lat, w_q_up=w_q_up, g_kv_lat=g_kv_lat, w_kv_up=w_kv_up, g_q_head=g_q_head,
                   g_k_head=g_k_head, g_sgu=g_sgu, w_spatial=w_spatial, b_spatial=b_spatial, w_conv=w_conv,
                   g_out=g_out, w_mix_out=w_mix_out, g_ffn2=g_ffn2, w_ffn2_in=w_ffn2_in, w_ffn2_out=w_ffn2_out)
    bsz, t, d = x.shape
    depth = w_ada.shape[0]
    rows = 8 * ((bsz + 1 + 7) // 8)
    cvec = jnp.zeros((rows, d), F32).at[:bsz].set(c).at[bsz].set(c_ctx)
    mod = _ada(cvec, w_ada, b_ada)
    lat_tables = _rope_tables(t)
    ctx_tables = _identity_tables(ctx.shape[1])

    h, hc = x, ctx
    ffn1_in, ffn1_out = _cast_bf16(w_ffn1_in), _cast_bf16(w_ffn1_out)
    ffn2_in, ffn2_out = _cast_bf16(w_ffn2_in), _cast_bf16(w_ffn2_out)
    for l in range(depth):
        last = l == depth - 1
        p = _prep_layer(l, weights)
        mod_l = mod[l, :bsz].reshape(bsz, ADA_CHUNKS, d)
        mod_c1 = mod[l, bsz].reshape(1, ADA_CHUNKS, d)
        mod_c = jnp.broadcast_to(mod_c1, (bsz, ADA_CHUNKS, d))
        flat = lambda a: a.reshape(1, -1, a.shape[-1])

        ffn1 = functools.partial(_ffn, g=p["g_ffn1"], w_in=ffn1_in, wout=ffn1_out, layer=l, mod_base=0)
        ffn2 = functools.partial(_ffn, g=p["g_ffn2"], w_in=ffn2_in, wout=ffn2_out, layer=l, mod_base=6)

        h = ffn1(h, mod_l)
        hc = ffn1(flat(hc), mod_c1).reshape(ctx.shape)

        q, k, v, sgcv = _mix(h, mod_l, p, lat_tables)
        qc, kc, vc, sgcv_c = _mix(hc, mod_c, p, ctx_tables)
        attn = lax.cond(_max_shift(g_q_head[l], g_k_head[l]) <= MAX_SAFE_SHIFT,
                        functools.partial(_attention, online=False),
                        functools.partial(_attention, online=True), q, k, v, kc, vc)
        h = ffn2(h, mod_l, merge_args=(attn, sgcv, p["g_out_att"], p["w_mix_out"]))
        if not last:
            attn_c = _attention(qc, kc, vc, online=True)
            hc = ffn2(flat(hc), mod_c1,
                      merge_args=(flat(attn_c), flat(sgcv_c), p["g_out_att"], p["w_mix_out"])).reshape(ctx.shape)
    return h
```

```python
import functools

import jax
import jax.numpy as jnp
from jax import lax
from jax.experimental import pallas as pl
from jax.experimental.pallas import tpu as pltpu

EPS = 1e-6
ADA_CHUNKS = 9
FFN_RESIDUAL_WEIGHT = 0.5
GRID_W = 64
ROPE_BASE = 10000.0
MLA_HEADS = 8
NOPE_DIM = 64
ROPE_DIM = 32
AXIS_DIM = ROPE_DIM // 2
QK_HEAD_DIM = NOPE_DIM + ROPE_DIM
V_HEAD_DIM = 64
Q_LORA = 384
KV_LORA = 256
ATT_WIDTH = MLA_HEADS * V_HEAD_DIM
SG_GROUPS = 4
SG_GROUP_DIM = 64
SG_WIDTH = SG_GROUPS * SG_GROUP_DIM
CHUNK = 128
CONV_WIDTH = 256
OFF_KV = Q_LORA
OFF_KPE = OFF_KV + KV_LORA
OFF_SG = OFF_KPE + ROPE_DIM
LOG2E = 1.4426950408889634
SHIFT_LANE = QK_HEAD_DIM
MAX_SAFE_SHIFT = 60.0

LANES = 128
BF16_ROWS = 16
VMEM_CAPACITY_BYTES = 64 * 1024 * 1024
VMEM_LIMIT_BYTES = VMEM_CAPACITY_BYTES * 7 // 8

MIX_Q = 0
MIX_KPE = MIX_Q + Q_LORA
MIX_KV = MIX_KPE + LANES
MIX_SG = MIX_KV + KV_LORA
MIX_CV = MIX_SG + 2 * SG_WIDTH

FF_CHUNK = 256
TOKEN_TILE = 1024
FFN_TILE = 1024
FFN_SUBTILE = 512
Q_TILE = 512
KV_TILE = 256

BF16 = jnp.bfloat16
F32 = jnp.float32


def _dot(a, b):
    return jnp.dot(a, b, preferred_element_type=F32)


def _rms(x, g):
    return x * lax.rsqrt(jnp.mean(x * x, axis=-1, keepdims=True) + EPS) * g


def _silu(x):
    return x * jax.nn.sigmoid(x)


def _gelu_tanh(x):
    return 0.5 * x * (1.0 + jnp.tanh(0.7978845608028654 * (x + 0.044715 * (x * x * x))))


def _const_spec(shape):
    zeros = (0,) * len(shape)
    return pl.BlockSpec(shape, lambda *_: zeros, pipeline_mode=pl.Buffered(1))


def _ada_kernel(c_ref, w_ref, b_ref, o_ref):
    s = _silu(c_ref[...])
    w = w_ref[0]
    s_hi = s.astype(BF16)
    s_lo = (s - s_hi.astype(F32)).astype(BF16)
    w_hi = w.astype(BF16)
    w_lo = (w - w_hi.astype(F32)).astype(BF16)
    o_ref[0] = _dot(s_hi, w_hi) + _dot(s_lo, w_hi) + _dot(s_hi, w_lo) + b_ref[0]


def _ada(cvec, w_ada, b_ada):
    depth, d, n = w_ada.shape
    rows = cvec.shape[0]
    tn = d
    return pl.pallas_call(
        _ada_kernel,
        grid=(depth, n // tn),
        in_specs=[
            pl.BlockSpec((rows, d), lambda l, j: (0, 0)),
            pl.BlockSpec((1, d, tn), lambda l, j: (l, 0, j)),
            pl.BlockSpec((1, 1, tn), lambda l, j: (l, 0, j)),
        ],
        out_specs=pl.BlockSpec((1, rows, tn), lambda l, j: (l, 0, j)),
        out_shape=jax.ShapeDtypeStruct((depth, rows, n), F32),
        compiler_params=pltpu.CompilerParams(vmem_limit_bytes=VMEM_LIMIT_BYTES),
        name="ada_modulation",
    )(cvec, w_ada, b_ada.reshape(depth, 1, n))


def _ffn_kernel(*refs, merge, mod_base):
    if merge:
        (x_ref, mod_ref, attn_ref, sgcv_ref, gatt_ref, wmo_ref, g_ref, win_ref, wout_ref,
         o_ref, h_scr, acc_scr) = refs
    else:
        x_ref, mod_ref, g_ref, win_ref, wout_ref, o_ref, h_scr, acc_scr = refs
    d_ff = wout_ref.shape[0]
    tm = x_ref.shape[1]
    sub = min(FFN_SUBTILE, tm)
    subs = [slice(r, r + sub) for r in range(0, tm, sub)]
    shift = mod_ref[0, mod_base:mod_base + 1, :]
    scale = mod_ref[0, mod_base + 1:mod_base + 2, :]
    for rows in subs:
        x = x_ref[0, rows, :]
        if merge:
            att = _rms(attn_ref[0, rows, :], gatt_ref[...]).astype(BF16)
            y = _dot(att, wmo_ref[0:ATT_WIDTH, :]) + _dot(sgcv_ref[0, rows, :], wmo_ref[ATT_WIDTH:, :])
            x = x + mod_ref[0, 5:6, :] * y
        o_ref[0, rows, :] = x
        h_scr[rows, :] = (_rms(x, g_ref[...]) * (1.0 + scale) + shift).astype(BF16)

    def chunk(c, rows):
        a = _dot(h_scr[rows, :], win_ref[:, c:c + FF_CHUNK])
        b = _dot(h_scr[rows, :], win_ref[:, d_ff + c:d_ff + c + FF_CHUNK])
        return _dot((_silu(a) * b).astype(BF16), wout_ref[c:c + FF_CHUNK, :])

    for c in range(0, d_ff, FF_CHUNK):
        for rows in subs:
            if c == 0:
                acc_scr[rows, :] = chunk(c, rows)
            else:
                acc_scr[rows, :] += chunk(c, rows)
    gate = mod_ref[0, mod_base + 2:mod_base + 3, :]
    for rows in subs:
        o_ref[0, rows, :] = o_ref[0, rows, :] + (FFN_RESIDUAL_WEIGHT * gate) * acc_scr[rows, :]


def _ffn(x, mod, g, w_in, wout, *, layer, mod_base, merge_args=None):
    bsz, t, d = x.shape
    tm = min(FFN_TILE, t)
    assert wout.shape[1] % FF_CHUNK == 0
    slab = lambda w: pl.BlockSpec((None,) + w.shape[1:], lambda b, i: (layer, 0, 0), pipeline_mode=pl.Buffered(1))
    tile = lambda w: pl.BlockSpec((1, tm, w), lambda b, i: (b, i, 0))
    in_specs = [tile(d), pl.BlockSpec((1, ADA_CHUNKS, d), lambda b, i: (b, 0, 0))]
    args = [x, mod]
    if merge_args is not None:
        attn, sgcv, g_att, wmo = merge_args
        in_specs += [tile(ATT_WIDTH), tile(SG_WIDTH + CONV_WIDTH), _const_spec(g_att.shape), _const_spec(wmo.shape)]
        args += [attn, sgcv, g_att, wmo]
    in_specs += [_const_spec(g.shape), slab(w_in), slab(wout)]
    args += [g, w_in, wout]
    return pl.pallas_call(
        functools.partial(_ffn_kernel, merge=merge_args is not None, mod_base=mod_base),
        grid=(bsz, t // tm),
        in_specs=in_specs,
        out_specs=tile(d),
        out_shape=jax.ShapeDtypeStruct(x.shape, F32),
        scratch_shapes=[pltpu.VMEM((tm, d), BF16), pltpu.VMEM((tm, d), F32)],
        compiler_params=pltpu.CompilerParams(vmem_limit_bytes=VMEM_LIMIT_BYTES),
        name="ffn_merge" if merge_args is not None else "ffn",
    )(*args)


def _rope(xg, cos, sin_next, sin_prev):
    return xg * cos + pltpu.roll(xg, LANES - 8, 1) * sin_next + pltpu.roll(xg, 8, 1) * sin_prev


def _mix_kernel(x_ref, xp_ref, xn_ref, mod_ref, gmix_ref, w_ref, gql_ref, wqt_ref, gkvl_ref, wk_ref, wvt_ref,
                gqc_ref, gks_ref, gkpe_ref, shift_ref, vone_ref, cos_ref, sn_ref, sp_ref, cost_ref, sint_ref,
                gsgu_ref, wsp_ref, bsp_ref,
                wcv_ref, gosg_ref, gocv_ref, q_ref, k_ref, v_ref, sgcv_ref, h_scr, z_scr, *, tm, q_scale):
    i = pl.program_id(1)
    last = pl.num_programs(1) - 1
    halo = BF16_ROWS
    shift = mod_ref[0, 3:4, :]
    scale1 = 1.0 + mod_ref[0, 4:5, :]
    gmix = gmix_ref[...]

    def norm_mod(x):
        return (_rms(x, gmix) * scale1 + shift).astype(BF16)

    h_scr[0:halo] = norm_mod(xp_ref[0])
    h_scr[halo:halo + tm] = norm_mod(x_ref[0])
    h_scr[halo + tm:] = norm_mod(xn_ref[0])
    hm = h_scr[halo:halo + tm]

    cos = cos_ref[...]
    sn = sn_ref[...]
    sp = sp_ref[...]

    q_kpe = _dot(hm, w_ref[:, MIX_Q:MIX_KV])
    kvlat = _dot(hm, w_ref[:, MIX_KV:MIX_KV + KV_LORA])
    sgin = _dot(hm, w_ref[:, MIX_SG:MIX_SG + 2 * SG_WIDTH])
    cvin = _dot(h_scr[...], w_ref[:, MIX_CV:MIX_CV + 3 * CONV_WIDTH])

    qn = _rms(q_kpe[:, :Q_LORA], gql_ref[...]).astype(BF16)
    kvn = _rms(kvlat, gkvl_ref[...]).astype(BF16)
    qst = lax.dot_general(wqt_ref[...], qn, (((1,), (1,)), ((), ())), preferred_element_type=F32)
    kk = _dot(kvn, wk_ref[...])
    vt = lax.dot_general(wvt_ref[...], kvn, (((1,), (1,)), ((), ())), preferred_element_type=F32)
    cost = cost_ref[...]
    sint = sint_ref[...]
    gq = gqc_ref[...]
    lane = lax.broadcasted_iota(jnp.int32, (1, LANES), 1)
    half = AXIS_DIM // 2
    q_tail = jnp.broadcast_to(shift_ref[...], (LANES - QK_HEAD_DIM, tm)).astype(BF16)
    v_tail = jnp.broadcast_to(vone_ref[...], (LANES - V_HEAD_DIM, tm)).astype(BF16)
    for h in range(MLA_HEADS):
        xq = qst[h * QK_HEAD_DIM:(h + 1) * QK_HEAD_DIM]
        r = lax.rsqrt(jnp.sum(xq * xq, axis=0, keepdims=True) * (1.0 / QK_HEAD_DIM) + EPS) * q_scale
        xg = xq * gq
        pe = xg[NOPE_DIM:]
        swapped = jnp.concatenate([pe[half:2 * half], pe[0:half], pe[3 * half:4 * half], pe[2 * half:3 * half]], axis=0)
        y = jnp.concatenate([xg[0:NOPE_DIM], pe * cost + swapped * sint], axis=0)
        q_ref[0, h, 0:QK_HEAD_DIM, :] = (y * r).astype(BF16)
        q_ref[0, h, QK_HEAD_DIM:, :] = q_tail
        v_ref[0, h, 0:V_HEAD_DIM, :] = vt[h * V_HEAD_DIM:(h + 1) * V_HEAD_DIM].astype(BF16)
        v_ref[0, h, V_HEAD_DIM:, :] = v_tail

    kpe = q_kpe[:, Q_LORA:]
    ss_pe = jnp.sum(kpe * kpe, axis=-1, keepdims=True)
    pe = _rope(kpe * gkpe_ref[...], cos, sn, sp)
    k_one = jnp.where(lane == SHIFT_LANE, 1.0, 0.0)
    for h in range(MLA_HEADS):
        sl = slice(h * LANES, (h + 1) * LANES)
        xk = kk[:, sl]
        r = lax.rsqrt((jnp.sum(xk * xk, axis=-1, keepdims=True) + ss_pe) * (1.0 / QK_HEAD_DIM) + EPS)
        k_ref[0, h] = ((xk * gks_ref[:, sl] + pe) * r + k_one).astype(BF16)

    ge = _gelu_tanh(sgin)
    u = ge[:, :SG_WIDTH]
    low = lane < SG_GROUP_DIM
    vparts = []
    for c in range(SG_WIDTH // LANES):
        vv = ge[:, SG_WIDTH + c * LANES:SG_WIDTH + (c + 1) * LANES]
        v2 = vv * vv
        s_lo = jnp.sum(jnp.where(low, v2, 0.0), axis=-1, keepdims=True)
        s_hi = jnp.sum(jnp.where(low, 0.0, v2), axis=-1, keepdims=True)
        r = jnp.where(low, lax.rsqrt(s_lo * (1.0 / SG_GROUP_DIM) + EPS), lax.rsqrt(s_hi * (1.0 / SG_GROUP_DIM) + EPS))
        vparts.append(vv * r * gsgu_ref[:, c * LANES:(c + 1) * LANES])
    vn = jnp.concatenate(vparts, axis=-1)
    group = lax.broadcasted_iota(jnp.int32, (1, SG_WIDTH), 1) // SG_GROUP_DIM
    sg_rows = []
    for c in range(tm // CHUNK):
        vc = vn[c * CHUNK:(c + 1) * CHUNK]
        stacked = jnp.concatenate([jnp.where(group == g, vc, 0.0) for g in range(SG_GROUPS)], axis=0)
        vs = _dot(wsp_ref[...], stacked.astype(BF16)) + bsp_ref[...]
        sg_rows.append(u[c * CHUNK:(c + 1) * CHUNK] * vs)
    sg = jnp.concatenate(sg_rows, axis=0)
    sgcv_ref[0, :, 0:SG_WIDTH] = _rms(sg, gosg_ref[...]).astype(BF16)

    z =cvin[:, CONV_WIDTH:2 * CONV_WIDTH] * cvin[:, 2 * CONV_WIDTH:]
    z_scr[...] = z
    z_scr[0:halo] = jnp.where(i > 0, z[0:halo], 0.0)
    z_scr[halo + tm:] = jnp.where(i < last, z[halo + tm:], 0.0)
    y = (wcv_ref[0:1, :] * z_scr[halo - 1:halo - 1 + tm] + wcv_ref[1:2, :] * z_scr[halo:halo + tm]
         + wcv_ref[2:3, :] * z_scr[halo + 1:halo + 1 + tm])
    cv = cvin[halo:halo + tm, 0:CONV_WIDTH] * y
    sgcv_ref[0, :, SG_WIDTH:] = _rms(cv, gocv_ref[...]).astype(BF16)


def _mix(x, mod, p, tables):
    bsz, t, d = x.shape
    tm = min(TOKEN_TILE, t)
    halo = BF16_ROWS
    nblk = t // halo
    per = tm // halo
    cos, sn, sp, cost, sint = tables
    const_args = [p["g_mix"], p["w_mix"], p["g_q_lat"], p["w_qt"], p["g_kv_lat"], p["w_k"], p["w_vt"],
                  p["g_q_col"], p["g_k_slot"], p["g_k_pe"], p["q_shift"], p["v_one"]]
    tail_args = [p["g_sgu"], p["w_sp"], p["b_sp"], p["w_conv"], p["g_out_sg"], p["g_out_cv"]]
    tab_spec = pl.BlockSpec((tm, LANES), lambda b, i: (i, 0))
    tabt_spec = pl.BlockSpec((ROPE_DIM, tm), lambda b, i: (0, i))
    in_specs = ([pl.BlockSpec((1, tm, d), lambda b, i: (b, i, 0)),
                 pl.BlockSpec((1, halo, d), lambda b, i: (b, jnp.maximum(i * per - 1, 0), 0)),
                 pl.BlockSpec((1, halo, d), lambda b, i: (b, jnp.minimum((i + 1) * per, nblk - 1), 0)),
                 pl.BlockSpec((1, ADA_CHUNKS, d), lambda b, i: (b, 0, 0))]
                + [_const_spec(a.shape) for a in const_args]
                + [tab_spec, tab_spec, tab_spec, tabt_spec, tabt_spec]
                + [_const_spec(a.shape) for a in tail_args])
    head_spec = pl.BlockSpec((1, MLA_HEADS, tm, LANES), lambda b, i: (b, 0, i, 0))
    headt_spec = pl.BlockSpec((1, MLA_HEADS, LANES, tm), lambda b, i: (b, 0, 0, i))
    return pl.pallas_call(
        functools.partial(_mix_kernel, tm=tm, q_scale=LOG2E * QK_HEAD_DIM ** -0.5),
        grid=(bsz, t // tm),
        in_specs=in_specs,
        out_specs=[headt_spec, head_spec, headt_spec,
                   pl.BlockSpec((1, tm, SG_WIDTH + CONV_WIDTH), lambda b, i: (b, i, 0))],
        out_shape=[jax.ShapeDtypeStruct((bsz, MLA_HEADS, LANES, t), BF16),
                   jax.ShapeDtypeStruct((bsz, MLA_HEADS, t, LANES), BF16),
                   jax.ShapeDtypeStruct((bsz, MLA_HEADS, LANES, t), BF16),
                   jax.ShapeDtypeStruct((bsz, t, SG_WIDTH + CONV_WIDTH), BF16)],
        scratch_shapes=[pltpu.VMEM((tm + 2 * halo, d), BF16), pltpu.VMEM((tm + 2 * halo, CONV_WIDTH), F32)],
        compiler_params=pltpu.CompilerParams(vmem_limit_bytes=VMEM_LIMIT_BYTES),
        name="mixer_front",
    )(x, x, x, mod, *const_args, cos, sn, sp, cost, sint, *tail_args)


NEG_BIG = -1e30
VT_ROWS = LANES
HEADS_PER_STEP = 4
SCORE_LEAD = 1


def _attn_kernel(*refs, tk, has_extra, online):
    if has_extra:
        q_ref, k_ref, vt_ref, k2_ref, vt2_ref, o_ref = refs
    else:
        q_ref, k_ref, vt_ref, o_ref = refs
    tq = q_ref.shape[3]
    qs = [q_ref[0, h] for h in range(HEADS_PER_STEP)]

    def consume(st, vth, state):
        if online:
            m, acc = state
            m_new = jnp.maximum(m, jnp.max(st, axis=0, keepdims=True))
            acc = jnp.exp2(m - m_new) * acc + _dot(vth, jnp.exp2(st - m_new).astype(BF16))
            return m_new, acc
        (acc,) = state
        return (acc + _dot(vth, jnp.exp2(st).astype(BF16)),)

    chunks = []
    for j in range(k_ref.shape[2] // tk):
        ks = slice(j * tk, (j + 1) * tk)
        chunks.append((lambda h, ks=ks: k_ref[0, h, ks, :], lambda h, ks=ks: vt_ref[0, h, 0:VT_ROWS, ks]))
    if has_extra:
        chunks.append((lambda h: k2_ref[0, h], lambda h: vt2_ref[0, h, 0:VT_ROWS, :]))

    acc0 = jnp.zeros((VT_ROWS, tq), F32)
    init = (jnp.full((1, tq), NEG_BIG, F32), acc0) if online else (acc0,)
    states = [init] * HEADS_PER_STEP
    n = len(chunks)
    score = lambda j: [_dot(chunks[j][0](h), qs[h]) for h in range(HEADS_PER_STEP)]
    pending = [score(j) for j in range(min(SCORE_LEAD, n))]
    for j in range(n):
        if j + SCORE_LEAD < n:
            pending.append(score(j + SCORE_LEAD))
        scores = pending.pop(0)
        states = [consume(scores[h], chunks[j][1](h), states[h]) for h in range(HEADS_PER_STEP)]
    outs = []
    for st in states:
        acc = st[-1]
        outs.append(acc[0:V_HEAD_DIM] / acc[V_HEAD_DIM:V_HEAD_DIM + 1])
    o_ref[0] = jnp.concatenate(outs, axis=0).T


def _attention(q, k, v, k2=None, v2=None, *, online):
    bsz, heads, _, t = q.shape
    t1 = k.shape[2]
    tq = min(Q_TILE, t)
    tk = min(KV_TILE, t1)
    has_extra = k2 is not None
    q_spec = pl.BlockSpec((1, HEADS_PER_STEP, LANES, tq), lambda b, p, i: (b, p, 0, i))
    k_spec = lambda n: pl.BlockSpec((1, HEADS_PER_STEP, n, LANES), lambda b, p, i: (b, p, 0, 0))
    vt_spec = lambda n: pl.BlockSpec((1, HEADS_PER_STEP, LANES, n), lambda b, p, i: (b, p, 0, 0))
    in_specs = [q_spec, k_spec(t1), vt_spec(t1)]
    args = [q, k, v]
    if has_extra:
        in_specs += [k_spec(k2.shape[2]), vt_spec(k2.shape[2])]
        args += [k2, v2]
    return pl.pallas_call(
        functools.partial(_attn_kernel, tk=tk, has_extra=has_extra, online=online),
        grid=(bsz, heads // HEADS_PER_STEP, t // tq),
        in_specs=in_specs,
        out_specs=pl.BlockSpec((1, tq, HEADS_PER_STEP * V_HEAD_DIM), lambda b, p, i: (b, i, p)),
        out_shape=jax.ShapeDtypeStruct((bsz, t, ATT_WIDTH), F32),
        compiler_params=pltpu.CompilerParams(vmem_limit_bytes=VMEM_LIMIT_BYTES),
        name=("attention_online" if online else "attention_shifted") + ("_extra" if has_extra else ""),
    )(*args)


def _slot_pad(w, width):
    lead = w.shape[:-1]
    w = w.reshape(*lead, MLA_HEADS, width)
    w = jnp.pad(w, [(0, 0)] * len(lead) + [(0, 0), (0, LANES - width)])
    return w.reshape(*lead, MLA_HEADS * LANES)


CAST_BLOCK_BYTES = 4 * 1024 * 1024


def _cast_kernel(x_ref, o_ref):
    o_ref[...] = x_ref[...].astype(BF16)


def _cast_bf16(w):
    depth, r, c = w.shape
    rows = [tr for tr in range(BF16_ROWS, r + 1, BF16_ROWS) if r % tr == 0 and tr * c * 4 <= CAST_BLOCK_BYTES]
    tr = max(rows)
    spec = pl.BlockSpec((1, tr, c), lambda l, j: (l, j, 0))
    return pl.pallas_call(
        _cast_kernel, grid=(depth, r // tr), in_specs=[spec], out_specs=spec,
        out_shape=jax.ShapeDtypeStruct(w.shape, BF16),
        compiler_params=pltpu.CompilerParams(vmem_limit_bytes=VMEM_LIMIT_BYTES),
        name="cast_bf16",
    )(w)


def _max_shift(g_q, g_k):
    return (LOG2E * QK_HEAD_DIM ** 0.5) * jnp.max(jnp.abs(g_q)) * jnp.max(jnp.abs(g_k))


def _prep_layer(l, w):
    row = lambda a: a.reshape(1, -1).astype(F32)
    w_mix_in = w["w_mix_in"][l]
    d = w_mix_in.shape[0]
    kpe_cols = jnp.zeros((d, LANES), F32).at[:, NOPE_DIM:QK_HEAD_DIM].set(w_mix_in[:, OFF_KPE:OFF_SG])
    w_mix = jnp.concatenate([w_mix_in[:, :OFF_KV], kpe_cols, w_mix_in[:, OFF_KV:OFF_KPE], w_mix_in[:, OFF_SG:]],
                            axis=1).astype(BF16)
    w_kv = w["w_kv_up"][l].reshape(KV_LORA, MLA_HEADS, NOPE_DIM + V_HEAD_DIM)
    w_k = _slot_pad(w_kv[:, :, :NOPE_DIM].reshape(KV_LORA, -1), NOPE_DIM)
    w_v = w_kv[:, :, NOPE_DIM:].reshape(KV_LORA, ATT_WIDTH)
    g_k = w["g_k_head"][l]
    g_out = w["g_out"][l]
    w_sp = w["w_spatial"][l]
    return {
        "g_ffn1": row(w["g_ffn1"][l]), "g_ffn2": row(w["g_ffn2"][l]),
        "g_mix": row(w["g_mix"][l]), "w_mix": w_mix,
        "g_q_lat": row(w["g_q_lat"][l]), "w_qt": w["w_q_up"][l].T.astype(BF16),
        "g_kv_lat": row(w["g_kv_lat"][l]), "w_k": w_k.astype(BF16), "w_vt": w_v.T.astype(BF16),
        "g_q_col": w["g_q_head"][l].astype(F32).reshape(QK_HEAD_DIM, 1),
        "g_k_slot": row(_slot_pad(jnp.tile(g_k[:NOPE_DIM], MLA_HEADS), NOPE_DIM)),
        "g_k_pe": row(jnp.zeros((LANES,), F32).at[NOPE_DIM:QK_HEAD_DIM].set(g_k[NOPE_DIM:])),
        "q_shift": jnp.zeros((LANES - QK_HEAD_DIM, 1), F32).at[0, 0].set(-_max_shift(w["g_q_head"][l], g_k)),
        "v_one": jnp.zeros((LANES - V_HEAD_DIM, 1), F32).at[0, 0].set(1.0),
        "g_sgu": row(w["g_sgu"][l]),
        "w_sp": jnp.concatenate([w_sp[g] for g in range(SG_GROUPS)], axis=1).astype(BF16),
        "b_sp": jnp.repeat(w["b_spatial"][l].T, SG_GROUP_DIM, axis=1).astype(F32),
        "w_conv": w["w_conv"][l].astype(F32),
        "g_out_att": row(g_out[:ATT_WIDTH]),
        "g_out_sg": row(g_out[ATT_WIDTH:ATT_WIDTH + SG_WIDTH]),
        "g_out_cv": row(g_out[ATT_WIDTH + SG_WIDTH:]),
        "w_mix_out": w["w_mix_out"][l].astype(BF16),
    }


def _rope_tables(n):
    rows = n // GRID_W
    row = jnp.repeat(jnp.arange(rows), GRID_W).astype(F32)
    col = jnp.tile(jnp.arange(GRID_W), rows).astype(F32)
    inv = 1.0 / (ROPE_BASE ** (jnp.arange(0, AXIS_DIM, 2, dtype=F32) / AXIS_DIM))
    ang_r = row[:, None] * inv
    ang_c = col[:, None] * inv
    ang = jnp.concatenate([ang_r, ang_r, ang_c, ang_c], axis=-1)
    cos, sin = jnp.cos(ang), jnp.sin(ang)
    first_half = (jnp.arange(ROPE_DIM) % AXIS_DIM) < AXIS_DIM // 2
    pad = lambda a, fill: jnp.pad(a, ((0, 0), (NOPE_DIM, LANES - QK_HEAD_DIM)), constant_values=fill)
    return (pad(cos, 1.0), pad(jnp.where(first_half, -sin, 0.0), 0.0), pad(jnp.where(first_half, 0.0, sin), 0.0),
            cos.T, jnp.where(first_half, -sin, sin).T)


def _identity_tables(n):
    return (jnp.ones((n, LANES), F32), jnp.zeros((n, LANES), F32), jnp.zeros((n, LANES), F32),
            jnp.ones((ROPE_DIM, n), F32), jnp.zeros((ROPE_DIM, n), F32))


def kernel(x, c, ctx, c_ctx, w_ada, b_ada, g_ffn1, w_ffn1_in, w_ffn1_out, g_mix, w_mix_in, g_q_lat, w_q_up, g_kv_lat, w_kv_up, g_q_head, g_k_head, g_sgu, w_spatial, b_spatial, w_conv, g_out, w_mix_out, g_ffn2, w_ffn2_in, w_ffn2_out):
    weights = dict(g_ffn1=g_ffn1, w_ffn1_in=w_ffn1_in, w_ffn1_out=w_ffn1_out, g_mix=g_mix, w_mix_in=w_mix_in,
                   g_q_lat=g_q_lat, w_q_up=w_q_up, g_kv_lat=g_kv_lat, w_kv_up=w_kv_up, g_q_head=g_q_head,
                   g_k_head=g_k_head, g_sgu=g_sgu, w_spatial=w_spatial, b_spatial=b_spatial, w_conv=w_conv,
                   g_out=g_out, w_mix_out=w_mix_out, g_ffn2=g_ffn2, w_ffn2_in=w_ffn2_in, w_ffn2_out=w_ffn2_out)
    bsz, t, d = x.shape
    depth = w_ada.shape[0]
    rows = 8 * ((bsz + 1 + 7) // 8)
    cvec = jnp.zeros((rows, d), F32).at[:bsz].set(c).at[bsz].set(c_ctx)
    mod = _ada(cvec, w_ada, b_ada)
    lat_tables = _rope_tables(t)
    ctx_tables = _identity_tables(ctx.shape[1])

    h, hc = x, ctx
    ffn1_in, ffn1_out = _cast_bf16(w_ffn1_in), _cast_bf16(w_ffn1_out)
    ffn2_in, ffn2_out = _cast_bf16(w_ffn2_in), _cast_bf16(w_ffn2_out)
    for l in range(depth):
        last = l == depth - 1
        p = _prep_layer(l, weights)
        mod_l = mod[l, :bsz].reshape(bsz, ADA_CHUNKS, d)
        mod_c1 = mod[l, bsz].reshape(1, ADA_CHUNKS, d)
        mod_c = jnp.broadcast_to(mod_c1, (bsz, ADA_CHUNKS, d))
        flat = lambda a: a.reshape(1, -1, a.shape[-1])

        ffn1 = functools.partial(_ffn, g=p["g_ffn1"], w_in=ffn1_in, wout=ffn1_out, layer=l, mod_base=0)
        ffn2 = functools.partial(_ffn, g=p["g_ffn2"], w_in=ffn2_in, wout=ffn2_out, layer=l, mod_base=6)

        h = ffn1(h, mod_l)
        hc = ffn1(flat(hc), mod_c1).reshape(ctx.shape)

        q, k, v, sgcv = _mix(h, mod_l, p, lat_tables)
        qc, kc, vc, sgcv_c = _mix(hc, mod_c, p, ctx_tables)
        attn = lax.cond(_max_shift(g_q_head[l], g_k_head[l]) <= MAX_SAFE_SHIFT,
                        functools.partial(_attention, online=False),
                        functools.partial(_attention, online=True), q, k, v, kc, vc)
        h = ffn2(h, mod_l, merge_args=(attn, sgcv, p["g_out_att"], p["w_mix_out"]))
        if not last:
            attn_c = _attention(qc, kc, vc, online=True)
            hc = ffn2(flat(hc), mod_c1,
                      merge_args=(flat(attn_c), flat(sgcv_c), p["g_out_att"], p["w_mix_out"])).reshape(ctx.shape)
    return h
```
